```python
import jax, jax.numpy as jnp
from jax import lax
import numpy as np

D_MODEL = 2048
BATCH = 2
SEQ = 8192
DEPTH = 4

MIX_WIDTH = D_MODEL
RW_HEAD_DIM = 64
RW_HEADS = (3 * D_MODEL // 8) // RW_HEAD_DIM
RW_WIDTH = RW_HEADS * RW_HEAD_DIM
RW_DECAY_LORA = 64
RW_AAA_LORA = 64
RW_GATE_LORA = 128
RW_GN_EPS = 64e-5
MLA_NOPE = 128
MLA_ROPE = 64
MLA_V = 128
MLA_HEADS = (3 * D_MODEL // 8) // MLA_V
MLA_WIDTH = MLA_HEADS * MLA_V
MLA_Q_LORA = 384
MLA_KV_LORA = 256
MLA_SCALE = (MLA_NOPE + MLA_ROPE) ** -0.5
ROPE_THETA = 10000.0
Q_BLOCK = 128
CONV_CH = MIX_WIDTH - RW_WIDTH - MLA_WIDTH
CONV_K = 31
RW_COLS = 3 * RW_WIDTH + RW_DECAY_LORA + RW_AAA_LORA + RW_GATE_LORA
MLA_COLS = MLA_Q_LORA + MLA_KV_LORA + MLA_ROPE
CONV_COLS = 2 * CONV_CH
IN_COLS = RW_COLS + MLA_COLS + CONV_COLS
RW_SPLITS = [RW_WIDTH, 2 * RW_WIDTH, 3 * RW_WIDTH, 3 * RW_WIDTH + RW_DECAY_LORA,
             3 * RW_WIDTH + RW_DECAY_LORA + RW_AAA_LORA]
FFN_DIM = 256 * ((8 * D_MODEL // 3 + 255) // 256)
FFN_RES = 0.5
N_MOD = 9
ADA_INIT = 0.25
LN_EPS = 1e-5
RMS_EPS = 1e-6
DEEPNORM_ALPHA = (2 * DEPTH) ** 0.25
DEEPNORM_BETA = (8 * DEPTH) ** -0.25

kernel_name = "hybrid_rwkv7_mla_conv_macaron_deepnorm"


def layer_norm(x, g, b, eps=LN_EPS):
    xf = x.astype(jnp.float32)
    mean = jnp.mean(xf, -1, keepdims=True)
    var = jnp.mean(jnp.square(xf - mean), -1, keepdims=True)
    return ((xf - mean) * lax.rsqrt(var + eps)).astype(x.dtype) * g + b


def rms_norm(x, g, eps=RMS_EPS):
    xf = x.astype(jnp.float32)
    return (xf * lax.rsqrt(jnp.mean(jnp.square(xf), -1, keepdims=True) + eps)).astype(x.dtype) * g


def time_shift(t):
    return jnp.pad(t, ((0, 0), (1, 0), (0, 0)))[:, :-1]


def apply_rope(t, cos, sin):
    half = t.shape[-1] // 2
    t1 = t[..., :half].astype(jnp.float32)
    t2 = t[..., half:].astype(jnp.float32)
    return jnp.concatenate([t1 * cos - t2 * sin, t1 * sin + t2 * cos], -1).astype(t.dtype)


def swiglu(h, w_in, w_out):
    gate, up = jnp.split(h @ w_in, 2, axis=-1)
    return (jax.nn.silu(gate) * up) @ w_out


def wkv7_scan(r, w, k, v, kk, a):
    B, _, H, N = r.shape

    def step(state, inp):
        r_t, w_t, k_t, v_t, kk_t, a_t = inp
        sa = jnp.einsum('bhvk,bhk->bhv', state, -kk_t)
        state = (state * w_t[:, :, None, :]
                 + sa[..., None] * (kk_t * a_t)[:, :, None, :]
                 + v_t[..., None] * k_t[:, :, None, :])
        return state, jnp.einsum('bhvk,bhk->bhv', state, r_t)

    xs = tuple(jnp.moveaxis(t.astype(jnp.float32), 1, 0) for t in (r, w, k, v, kk, a))
    state0 = jnp.zeros((B, H, N, N), jnp.float32)
    _, ys = lax.scan(step, state0, xs)
    return jnp.moveaxis(ys, 0, 1)


def rwkv7_group(p, mu, w0, w2, a0, a2, g2, k_k, k_a, r_k, lnx_g, lnx_b):
    B, S, _ = p.shape
    p = p + (time_shift(p) - p) * mu
    r, k, v, w_lo, a_lo, g_lo = jnp.split(p, RW_SPLITS, axis=-1)
    log_w = -jax.nn.softplus(-(w0 + jnp.tanh(w_lo) @ w2)) - 0.5
    decay = jnp.exp(-jnp.exp(log_w.astype(jnp.float32)))
    a = jax.nn.sigmoid(a0 + a_lo @ a2)
    g = jax.nn.sigmoid(g_lo) @ g2

    def heads(t):
        return t.reshape(B, S, RW_HEADS, RW_HEAD_DIM)

    kk = heads(k * k_k).astype(jnp.float32)
    kk = kk / jnp.maximum(jnp.linalg.norm(kk, axis=-1, keepdims=True), 1e-12)
    k = k * (1 + (a - 1) * k_a)
    r_h, k_h, v_h = heads(r), heads(k), heads(v)
    o = wkv7_scan(r_h, heads(decay), k_h, v_h, kk, heads(a))
    mean = jnp.mean(o, -1, keepdims=True)
    var = jnp.mean(jnp.square(o - mean), -1, keepdims=True)
    o = ((o - mean) * lax.rsqrt(var + RW_GN_EPS)).reshape(B, S, RW_WIDTH).astype(p.dtype) * lnx_g + lnx_b
    bonus = jnp.sum(r_h * k_h * r_k, -1, keepdims=True) * v_h
    return (o + bonus.reshape(B, S, RW_WIDTH)) * g


def causal_block_attention(q_nope, q_pe, k_nope, k_pe, v):
    B, S, H, _ = q_nope.shape
    nb = S // Q_BLOCK
    key_pos = jnp.arange(S)

    def to_blocks(t):
        return jnp.moveaxis(t.reshape((B, nb, Q_BLOCK) + t.shape[2:]), 1, 0)

    def one_block(args):
        qn, qp, start = args
        s = (jnp.einsum('bqhd,bkhd->bhqk', qn, k_nope)
             + jnp.einsum('bqhd,bkd->bhqk', qp, k_pe)).astype(jnp.float32) * MLA_SCALE
        q_pos = start + jnp.arange(Q_BLOCK)
        s = jnp.where(q_pos[:, None] >= key_pos[None, :], s, -jnp.inf)
        prob = jax.nn.softmax(s, axis=-1).astype(v.dtype)
        return jnp.einsum('bhqk,bkhd->bqhd', prob, v)

    o = lax.map(one_block, (to_blocks(q_nope), to_blocks(q_pe), jnp.arange(nb) * Q_BLOCK))
    return jnp.moveaxis(o, 0, 1).reshape(B, S, H * v.shape[-1])


def mla_group(p, cos, sin, q_norm_g, w_uq, kv_norm_g, w_ukv):
    B, S, _ = p.shape
    q_lat, kv_lat, k_pe = jnp.split(p, [MLA_Q_LORA, MLA_Q_LORA + MLA_KV_LORA], axis=-1)
    q = (rms_norm(q_lat, q_norm_g) @ w_uq).reshape(B, S, MLA_HEADS, MLA_NOPE + MLA_ROPE)
    kv = (rms_norm(kv_lat, kv_norm_g) @ w_ukv).reshape(B, S, MLA_HEADS, MLA_NOPE + MLA_V)
    q_nope, q_pe = q[..., :MLA_NOPE], q[..., MLA_NOPE:]
    k_nope, v = kv[..., :MLA_NOPE], kv[..., MLA_NOPE:]
    q_pe = apply_rope(q_pe, cos[:, :, None, :], sin[:, :, None, :])
    k_pe = apply_rope(k_pe, cos, sin)
    return causal_block_attention(q_nope, q_pe, k_nope, k_pe, v)


def conv_group(p, conv_w, conv_b, ln_g, ln_b):
    lin, gate = jnp.split(p, 2, axis=-1)
    u = lin * jax.nn.sigmoid(gate)
    u = lax.conv_general_dilated(u, conv_w[:, None, :], window_strides=(1,),
                                 padding=[(CONV_K - 1, 0)],
                                 dimension_numbers=('NWC', 'WIO', 'NWC'),
                                 feature_group_count=CONV_CH) + conv_b
    return jax.nn.silu(layer_norm(u, ln_g, ln_b))


def setup_inputs(seed: int = 0) -> dict:
    key = jax.random.key(seed)
    ks = iter(jax.random.split(key, 40))
    f32 = jnp.float32

    def nrm(shape, scale):
        return jax.random.normal(next(ks), shape, f32) * scale

    x = nrm((BATCH, SEQ, D_MODEL), 1.0)
    c = nrm((BATCH, D_MODEL), 1.0)
    positions = (jax.random.randint(next(ks), (BATCH, 1), 0, 1024, dtype=jnp.int32)
                 + jnp.arange(SEQ, dtype=jnp.int32)[None, :])
    w_ada = nrm((DEPTH, D_MODEL, N_MOD * D_MODEL), ADA_INIT * D_MODEL ** -0.5)
    b_ada = nrm((DEPTH, N_MOD * D_MODEL), 0.02)
    ln_g = 1.0 + nrm((DEPTH, 3, D_MODEL), 0.02)
    ln_b = nrm((DEPTH, 3, D_MODEL), 0.02)
    w_ffn1_in = nrm((DEPTH, D_MODEL, 2 * FFN_DIM), D_MODEL ** -0.5)
    w_ffn1_out = nrm((DEPTH, FFN_DIM, D_MODEL), DEEPNORM_BETA * FFN_DIM ** -0.5)
    w_ffn2_in = nrm((DEPTH, D_MODEL, 2 * FFN_DIM), D_MODEL ** -0.5)
    w_ffn2_out = nrm((DEPTH, FFN_DIM, D_MODEL), DEEPNORM_BETA * FFN_DIM ** -0.5)
    w_in = nrm((DEPTH, D_MODEL, IN_COLS), D_MODEL ** -0.5)
    w_out = nrm((DEPTH, MIX_WIDTH, D_MODEL), DEEPNORM_BETA * MIX_WIDTH ** -0.5)
    rw_mu = jax.random.uniform(next(ks), (DEPTH, RW_COLS), f32)
    rw_w0 = jnp.linspace(-6.5, -1.5, RW_WIDTH, dtype=f32)[None, :] + nrm((DEPTH, RW_WIDTH), 0.1)
    rw_w2 = nrm((DEPTH, RW_DECAY_LORA, RW_WIDTH), 0.1 * RW_DECAY_LORA ** -0.5)
    rw_a0 = nrm((DEPTH, RW_WIDTH), 0.1)
    rw_a2 = nrm((DEPTH, RW_AAA_LORA, RW_WIDTH), 0.1 * RW_AAA_LORA ** -0.5)
    rw_g2 = nrm((DEPTH, RW_GATE_LORA, RW_WIDTH), RW_GATE_LORA ** -0.5)
    rw_k_k = 0.85 + nrm((DEPTH, RW_WIDTH), 0.02)
    rw_k_a = 1.0 + nrm((DEPTH, RW_WIDTH), 0.02)
    rw_r_k = nrm((DEPTH, RW_HEADS, RW_HEAD_DIM), 0.1)
    rw_lnx_g = 1.0 + nrm((DEPTH, RW_WIDTH), 0.02)
    rw_lnx_b = nrm((DEPTH, RW_WIDTH), 0.02)
    mla_q_norm_g = 1.0 + nrm((DEPTH, MLA_Q_LORA), 0.02)
    mla_w_uq = nrm((DEPTH, MLA_Q_LORA, MLA_HEADS * (MLA_NOPE + MLA_ROPE)), MLA_Q_LORA ** -0.5)
    mla_kv_norm_g = 1.0 + nrm((DEPTH, MLA_KV_LORA), 0.02)
    mla_w_ukv = nrm((DEPTH, MLA_KV_LORA, MLA_HEADS * (MLA_NOPE + MLA_V)), MLA_KV_LORA ** -0.5)
    conv_w = nrm((DEPTH, CONV_K, CONV_CH), CONV_K ** -0.5)
    conv_b = nrm((DEPTH, CONV_CH), 0.02)
    conv_ln_g = 1.0 + nrm((DEPTH, CONV_CH), 0.02)
    conv_ln_b = nrm((DEPTH, CONV_CH), 0.02)
    return {"x": x, "c": c, "positions": positions, "w_ada": w_ada, "b_ada": b_ada,
            "ln_g": ln_g, "ln_b": ln_b, "w_ffn1_in": w_ffn1_in, "w_ffn1_out": w_ffn1_out,
            "w_ffn2_in": w_ffn2_in, "w_ffn2_out": w_ffn2_out, "w_in": w_in, "w_out": w_out,
            "rw_mu": rw_mu, "rw_w0": rw_w0, "rw_w2": rw_w2, "rw_a0": rw_a0, "rw_a2": rw_a2,
            "rw_g2": rw_g2, "rw_k_k": rw_k_k, "rw_k_a": rw_k_a, "rw_r_k": rw_r_k,
            "rw_lnx_g": rw_lnx_g, "rw_lnx_b": rw_lnx_b, "mla_q_norm_g": mla_q_norm_g,
            "mla_w_uq": mla_w_uq, "mla_kv_norm_g": mla_kv_norm_g, "mla_w_ukv": mla_w_ukv,
            "conv_w": conv_w, "conv_b": conv_b, "conv_ln_g": conv_ln_g, "conv_ln_b": conv_ln_b}


def reference(x, c, positions, w_ada, b_ada, ln_g, ln_b, w_ffn1_in, w_ffn1_out, w_ffn2_in,
              w_ffn2_out, w_in, w_out, rw_mu, rw_w0, rw_w2, rw_a0, rw_a2, rw_g2, rw_k_k, rw_k_a,
              rw_r_k, rw_lnx_g, rw_lnx_b, mla_q_norm_g, mla_w_uq, mla_kv_norm_g, mla_w_ukv,
              conv_w, conv_b, conv_ln_g, conv_ln_b):
    B = x.shape[0]
    half = MLA_ROPE // 2
    inv_freq = ROPE_THETA ** (-jnp.arange(half, dtype=jnp.float32) / half)
    ang = positions.astype(jnp.float32)[..., None] * inv_freq
    cos, sin = jnp.cos(ang), jnp.sin(ang)
    c_act = jax.nn.silu(c)
    for l in range(DEPTH):
        mod = (c_act @ w_ada[l] + b_ada[l]).reshape(B, N_MOD, 1, D_MODEL)
        sh_f1, sc_f1, g_f1, sh_m, sc_m, g_m, sh_f2, sc_f2, g_f2 = [mod[:, i] for i in range(N_MOD)]
        h = x * (1 + sc_f1) + sh_f1
        x = layer_norm(DEEPNORM_ALPHA * x + FFN_RES * (1 + g_f1) * swiglu(h, w_ffn1_in[l], w_ffn1_out[l]),
                       ln_g[l, 0], ln_b[l, 0])
        h = x * (1 + sc_m) + sh_m
        p_rw, p_mla, p_conv = jnp.split(h @ w_in[l], [RW_COLS, RW_COLS + MLA_COLS], axis=-1)
        y_rw = rwkv7_group(p_rw, rw_mu[l], rw_w0[l], rw_w2[l], rw_a0[l], rw_a2[l], rw_g2[l],
                           rw_k_k[l], rw_k_a[l], rw_r_k[l], rw_lnx_g[l], rw_lnx_b[l])
        y_mla = mla_group(p_mla, cos, sin, mla_q_norm_g[l], mla_w_uq[l], mla_kv_norm_g[l], mla_w_ukv[l])
        y_conv = conv_group(p_conv, conv_w[l], conv_b[l], conv_ln_g[l], conv_ln_b[l])
        y = jnp.concatenate([y_rw, y_mla, y_conv], axis=-1) @ w_out[l]
        x = layer_norm(DEEPNORM_ALPHA * x + (1 + g_m) * y, ln_g[l, 1], ln_b[l, 1])
        h = x * (1 + sc_f2) + sh_f2
        x = layer_norm(DEEPNORM_ALPHA * x + FFN_RES * (1 + g_f2) * swiglu(h, w_ffn2_in[l], w_ffn2_out[l]),
                       ln_g[l, 2], ln_b[l, 2])
    return x
```

```python
import functools

import jax
import jax.numpy as jnp
from jax import lax
from jax.experimental import pallas as pl
from jax.experimental.pallas import tpu as pltpu

F32 = jnp.float32
BF16 = jnp.bfloat16
HIGHEST = lax.Precision.HIGHEST

LANES = 128
RW_HEAD_DIM = 64
RW_CHUNK = 64
RW_DECAY_LORA = 64
RW_AAA_LORA = 64
RW_GATE_LORA = 128
RW_GN_EPS = 64e-5
MLA_NOPE = 128
MLA_ROPE = 64
MLA_V = 128
MLA_Q_LORA = 384
MLA_KV_LORA = 256
MLA_QK_PAD = 256
MLA_SCALE = (MLA_NOPE + MLA_ROPE) ** -0.5
ROPE_THETA = 10000.0
CONV_K = 31
CONV_HALO = 32
N_MOD = 9
FFN_RES = 0.5
LN_EPS = 1e-5
RMS_EPS = 1e-6
VMEM_LIMIT = 52 * 1024 * 1024
TOKEN_TILE = 512
ATTN_TILE = 512
RW_BLOCK = 256
FFN_TILE = 512
SUBLANES = 8


def _cparams(*sem):
    return pltpu.CompilerParams(dimension_semantics=sem, vmem_limit_bytes=VMEM_LIMIT)


def _dot(a, b):
    return jnp.dot(a, b, preferred_element_type=F32)


def _dot_hi(a, b):
    return jnp.dot(a, b, preferred_element_type=F32, precision=HIGHEST)


def _dot_nt_hi(a, b):
    return lax.dot_general(a, b, (((1,), (1,)), ((), ())),
                           preferred_element_type=F32, precision=HIGHEST)


def _dot_nt(a, b):
    return lax.dot_general(a, b, (((1,), (1,)), ((), ())), preferred_element_type=F32)


def _sigmoid(x):
    return 1.0 / (1.0 + jnp.exp(-x))


def _silu(x):
    return x * _sigmoid(x)


def _layer_norm(y, g, b):
    mean = jnp.mean(y, -1, keepdims=True)
    d = y - mean
    var = jnp.mean(d * d, -1, keepdims=True)
    return d * lax.rsqrt(var + LN_EPS) * g + b


def _ada_kernel(c_ref, w_ref, b_ref, o_ref):
    o_ref[0] = _dot_hi(_silu(c_ref[...]), w_ref[0]) + b_ref[0]


def _ada(c, w_ada, b_ada):
    depth, d, n = w_ada.shape
    b = SUBLANES * pl.cdiv(c.shape[0], SUBLANES)
    c = jnp.pad(c, ((0, b - c.shape[0]), (0, 0)))
    tn = 1024
    return pl.pallas_call(
        _ada_kernel,
        grid=(depth, n // tn),
        in_specs=[pl.BlockSpec((b, d), lambda l, j: (0, 0)),
                  pl.BlockSpec((1, d, tn), lambda l, j: (l, 0, j)),
                  pl.BlockSpec((1, 1, tn), lambda l, j: (l, 0, j))],
        out_specs=pl.BlockSpec((1, b, tn), lambda l, j: (l, 0, j)),
        out_shape=jax.ShapeDtypeStruct((depth, b, n), F32),
        compiler_params=_cparams("parallel", "parallel"),
        name="ada",
    )(c, w_ada, b_ada.reshape(depth, 1, n))


def _ffn_kernel(x_ref, mod_ref, wg_ref, wu_ref, wo_ref, lng_ref, lnb_ref, o_ref,
                h_ref, acc_ref, *, mod_base, alpha):
    j = pl.program_id(1)

    @pl.when(j == 0)
    def _():
        sh = mod_ref[0, mod_base:mod_base + 1, :]
        sc = mod_ref[0, mod_base + 1:mod_base + 2, :]
        h_ref[...] = (x_ref[...] * (1.0 + sc) + sh).astype(BF16)
        acc_ref[...] = jnp.zeros_like(acc_ref)

    h = h_ref[...]
    gate = _dot(h, wg_ref[...])
    up = _dot(h, wu_ref[...])
    act = (_silu(gate) * up).astype(BF16)
    acc_ref[...] += _dot(act, wo_ref[...])

    @pl.when(j == pl.num_programs(1) - 1)
    def _():
        g = mod_ref[0, mod_base + 2:mod_base + 3, :]
        y = alpha * x_ref[...] + (FFN_RES * (1.0 + g)) * acc_ref[...]
        o_ref[...] = _layer_norm(y, lng_ref[...], lnb_ref[...])


def _ffn(x, mod, w_in, w_out, ln_g, ln_b, *, mod_base, alpha, seq):
    t, d = x.shape
    f = w_out.shape[0]
    tm = min(TOKEN_TILE, seq)
    tf = FFN_TILE
    nf = f // tf
    per_b = seq // tm
    kern = functools.partial(_ffn_kernel, mod_base=mod_base, alpha=alpha)
    return pl.pallas_call(
        kern,
        grid=(t // tm, nf),
        in_specs=[pl.BlockSpec((tm, d), lambda i, j: (i, 0)),
                  pl.BlockSpec((1, N_MOD, d), lambda i, j: (i // per_b, 0, 0)),
                  pl.BlockSpec((d, tf), lambda i, j: (0, j)),
                  pl.BlockSpec((d, tf), lambda i, j: (0, nf + j)),
                  pl.BlockSpec((tf, d), lambda i, j: (j, 0)),
                  pl.BlockSpec((1, d), lambda i, j: (0, 0)),
                  pl.BlockSpec((1, d), lambda i, j: (0, 0))],
        out_specs=pl.BlockSpec((tm, d), lambda i, j: (i, 0)),
        out_shape=jax.ShapeDtypeStruct((t, d), F32),
        scratch_shapes=[pltpu.VMEM((tm, d), BF16), pltpu.VMEM((tm, d), F32)],
        compiler_params=_cparams("parallel", "arbitrary"),
        name="ffn",
    )(x, mod, w_in, w_in, w_out, ln_g, ln_b)


def _mod_matmul_kernel(x_ref, mod_ref, w_ref, o_ref, *, mod_base):
    sh = mod_ref[0, mod_base:mod_base + 1, :]
    sc = mod_ref[0, mod_base + 1:mod_base + 2, :]
    h = (x_ref[...] * (1.0 + sc) + sh).astype(BF16)
    o_ref[...] = _dot(h, w_ref[...])


def _mod_matmul(x, mod, w, *, mod_base, seq, tn):
    t, d = x.shape
    n = w.shape[1]
    tm = min(TOKEN_TILE, seq)
    per_b = seq // tm
    kern = functools.partial(_mod_matmul_kernel, mod_base=mod_base)
    return pl.pallas_call(
        kern,
        grid=(n // tn, t // tm),
        in_specs=[pl.BlockSpec((tm, d), lambda j, i: (i, 0)),
                  pl.BlockSpec((1, N_MOD, d), lambda j, i: (i // per_b, 0, 0)),
                  pl.BlockSpec((d, tn), lambda j, i: (0, j))],
        out_specs=pl.BlockSpec((tm, tn), lambda j, i: (i, j)),
        out_shape=jax.ShapeDtypeStruct((t, n), F32),
        compiler_params=_cparams("parallel", "parallel"),
        name="in_proj",
    )(x, mod, w)


def _rwkv_kernel(r_ref, k_ref, v_ref, lo_ref, mur_ref, muk_ref, muv_ref, mulo_ref,
                 w0_ref, w2_ref, a0_ref, a2_ref, g2_ref, kk_ref, ka_ref, rk_ref,
                 lng_ref, lnb_ref, o_ref,
                 state_ref, pr_ref, pk_ref, pv_ref, plo_ref, obuf_ref, *, tb):
    j = pl.program_id(2)
    L = RW_CHUNK

    @pl.when(j == 0)
    def _():
        state_ref[...] = jnp.zeros_like(state_ref)
        pr_ref[...] = jnp.zeros_like(pr_ref)
        pk_ref[...] = jnp.zeros_like(pk_ref)
        pv_ref[...] = jnp.zeros_like(pv_ref)
        plo_ref[...] = jnp.zeros_like(plo_ref)

    def shift_mix(p, prev_ref, mu):
        row = lax.broadcasted_iota(jnp.int32, p.shape, 0)
        prev = jnp.where(row == 0, prev_ref[0:1, :], pltpu.roll(p, 1, axis=0))
        prev_ref[0:1, :] = p[tb - 1:tb, :]
        return p + (prev - p) * mu

    r = shift_mix(r_ref[0], pr_ref, mur_ref[...])
    k = shift_mix(k_ref[0], pk_ref, muk_ref[...])
    v = shift_mix(v_ref[0], pv_ref, muv_ref[...])
    lo = shift_mix(lo_ref[0], plo_ref, mulo_ref[...])
    w_lo = lo[:, :RW_DECAY_LORA]
    a_lo = lo[:, RW_DECAY_LORA:RW_DECAY_LORA + RW_AAA_LORA]
    g_lo = lo[:, RW_DECAY_LORA + RW_AAA_LORA:]

    z = w0_ref[...] + _dot_hi(jnp.tanh(w_lo), w2_ref[...])
    softplus_neg_z = jnp.maximum(-z, 0.0) + jnp.log(1.0 + jnp.exp(-jnp.abs(z)))
    log_decay = -jnp.exp(-softplus_neg_z - 0.5)
    a = _sigmoid(a0_ref[...] + _dot_hi(a_lo, a2_ref[...]))
    gate = _dot_hi(_sigmoid(g_lo), g2_ref[...])

    lane_r = lax.broadcasted_iota(jnp.int32, (LANES, LANES), 0)
    lane_c = lax.broadcasted_iota(jnp.int32, (LANES, LANES), 1)
    same_head = ((lane_r // RW_HEAD_DIM) == (lane_c // RW_HEAD_DIM)).astype(F32)

    kk = k * kk_ref[...]
    kk_norm = jnp.sqrt(_dot_hi(kk * kk, same_head))
    kk = kk / jnp.maximum(kk_norm, 1e-12)
    k = k * (1.0 + (a - 1.0) * ka_ref[...])

    stack_mask = ((lane_r // L) == (lane_c // RW_HEAD_DIM)).astype(F32)
    strict_lower = lane_r > lane_c
    lower = lane_r >= lane_c
    eye = lane_r == lane_c
    tri = (lax.broadcasted_iota(jnp.int32, (L, L), 0)
           >= lax.broadcasted_iota(jnp.int32, (L, L), 1)).astype(F32)

    def stack(x):
        return jnp.concatenate([x, x], axis=0) * stack_mask

    state = state_ref[...]
    for c in range(tb // L):
        rows = slice(c * L, (c + 1) * L)
        lw = log_decay[rows]
        cum = _dot_hi(tri, lw)
        w_cum = jnp.exp(cum)
        w_inv = jnp.exp(-cum)
        w_prev = jnp.exp(cum - lw)
        w_last = w_cum[L - 1:L, :]
        kk_c = kk[rows]
        a2_ = stack(-kk_c * w_prev)
        b2_ = stack(kk_c * a[rows] * w_inv)
        k2_ = stack(k[rows] * w_inv)
        r2_ = stack(r[rows] * w_cum)
        v2_ = stack(v[rows])

        a_ab = jnp.where(strict_lower, _dot_nt_hi(a2_, b2_), 0.0)
        a_ak = jnp.where(strict_lower, _dot_nt_hi(a2_, k2_), 0.0)
        a_rb = jnp.where(lower, _dot_nt_hi(r2_, b2_), 0.0)
        a_rk = jnp.where(lower, _dot_nt_hi(r2_, k2_), 0.0)

        x = jnp.concatenate([a2_, _dot_hi(a_ak, v2_)], axis=1)
        p = a_ab
        x = x + _dot_hi(p, x)
        n = 2
        while n < L:
            p = _dot_hi(p, p)
            x = x + _dot_hi(p, x)
            n *= 2

        zz = _dot_hi(a_rb, x)
        r_hat = r2_ + zz[:, :LANES]
        o_intra = zz[:, LANES:] + _dot_hi(a_rk, v2_)
        mn = _dot_hi((b2_ * w_last).T, x)
        m_mat = jnp.where(eye, w_last, 0.0) + mn[:, :LANES]
        n_mat = mn[:, LANES:] + _dot_hi((k2_ * w_last).T, v2_)

        o2 = _dot_hi(r_hat, state) + o_intra
        obuf_ref[rows, :] = o2[:L] + o2[L:]
        state = _dot_hi(m_mat, state) + n_mat
    state_ref[...] = state

    o = obuf_ref[...]
    inv_n = 1.0 / RW_HEAD_DIM
    mean = _dot_hi(o, same_head) * inv_n
    d = o - mean
    var = _dot_hi(d * d, same_head) * inv_n
    on = d * lax.rsqrt(var + RW_GN_EPS) * lng_ref[...] + lnb_ref[...]
    bonus = _dot_hi(r * k * rk_ref[...], same_head) * v
    o_ref[0] = ((on + bonus) * gate).astype(o_ref.dtype)


def _rwkv(p_rw, mu, w0, w2, a0, a2, g2, k_k, k_a, r_k, lnx_g, lnx_b, *, batch, seq):
    width = w0.shape[-1]
    pairs = width // LANES
    tb = min(RW_BLOCK, seq)
    lo_w = RW_DECAY_LORA + RW_AAA_LORA + RW_GATE_LORA
    lo_blk = 3 * width // lo_w
    p3 = p_rw.reshape(batch, seq, p_rw.shape[-1])

    def col(off):
        return lambda b, h, j: (b, j, off + h)

    def vec(off):
        return lambda b, h, j: (0, off + h)

    row = lambda a: a.reshape(1, -1)
    kern = functools.partial(_rwkv_kernel, tb=tb)
    out = pl.pallas_call(
        kern,
        grid=(batch, pairs, seq // tb),
        in_specs=[pl.BlockSpec((1, tb, LANES), col(0)),
                  pl.BlockSpec((1, tb, LANES), col(pairs)),
                  pl.BlockSpec((1, tb, LANES), col(2 * pairs)),
                  pl.BlockSpec((1, tb, lo_w), lambda b, h, j: (b, j, lo_blk)),
                  pl.BlockSpec((1, LANES), vec(0)),
                  pl.BlockSpec((1, LANES), vec(pairs)),
                  pl.BlockSpec((1, LANES), vec(2 * pairs)),
                  pl.BlockSpec((1, lo_w), lambda b, h, j: (0, lo_blk)),
                  pl.BlockSpec((1, LANES), vec(0)),
                  pl.BlockSpec((RW_DECAY_LORA, LANES), vec(0)),
                  pl.BlockSpec((1, LANES), vec(0)),
                  pl.BlockSpec((RW_AAA_LORA, LANES), vec(0)),
                  pl.BlockSpec((RW_GATE_LORA, LANES), vec(0)),
                  pl.BlockSpec((1, LANES), vec(0)),
                  pl.BlockSpec((1, LANES), vec(0)),
                  pl.BlockSpec((1, LANES), vec(0)),
                  pl.BlockSpec((1, LANES), vec(0)),
                  pl.BlockSpec((1, LANES), vec(0))],
        out_specs=pl.BlockSpec((1, tb, LANES), lambda b, h, j: (b, j, h)),
        out_shape=jax.ShapeDtypeStruct((batch, seq, width), BF16),
        scratch_shapes=[pltpu.VMEM((LANES, LANES), F32),
                        pltpu.VMEM((8, LANES), F32), pltpu.VMEM((8, LANES), F32),
                        pltpu.VMEM((8, LANES), F32), pltpu.VMEM((8, lo_w), F32),
                        pltpu.VMEM((tb, LANES), F32)],
        compiler_params=_cparams("parallel", "parallel", "arbitrary"),
        name="rwkv",
    )(p3, p3, p3, p3, row(mu), row(mu), row(mu), row(mu), row(w0), w2, row(a0), a2, g2,
      row(k_k), row(k_a), row(r_k), row(lnx_g), row(lnx_b))
    return out.reshape(batch * seq, width)


def _rope_table_kernel(pos_ref, freq_ref, cos_ref, sa_ref, sb_ref):
    half = MLA_ROPE // 2
    ang = pos_ref[...].astype(F32) * freq_ref[...]
    lane = lax.broadcasted_iota(jnp.int32, ang.shape, 1)
    cos = jnp.cos(ang)
    sin = jnp.sin(ang)
    cos_ref[...] = jnp.where(lane < MLA_ROPE, cos, 1.0)
    sa_ref[...] = jnp.where(lane < half, -sin, 0.0)
    sb_ref[...] = jnp.where((lane >= half) & (lane < MLA_ROPE), sin, 0.0)


def _rope_tables(positions):
    t = positions.size
    half = MLA_ROPE // 2
    tm = min(2048, t)
    inv_freq = ROPE_THETA ** (-jnp.arange(half, dtype=F32) / half)
    freq = jnp.concatenate([inv_freq, inv_freq, jnp.zeros((LANES - MLA_ROPE,), F32)])
    spec = pl.BlockSpec((tm, LANES), lambda i: (i, 0))
    shp = jax.ShapeDtypeStruct((t, LANES), F32)
    return pl.pallas_call(
        _rope_table_kernel,
        grid=(t // tm,),
        in_specs=[pl.BlockSpec((tm, 1), lambda i: (i, 0)),
                  pl.BlockSpec((1, LANES), lambda i: (0, 0))],
        out_specs=[spec, spec, spec],
        out_shape=[shp, shp, shp],
        compiler_params=_cparams("parallel"),
        name="rope_tables",
    )(positions.reshape(t, 1), freq.reshape(1, LANES))


def _rope(x, cos, sa, sb):
    half = MLA_ROPE // 2
    return x * cos + pltpu.roll(x, LANES - half, axis=1) * sa + pltpu.roll(x, half, axis=1) * sb


def _mla_up_kernel(p_ref, cos_ref, sa_ref, sb_ref, qg_ref, wq_ref, kvg_ref, wk_ref, wv_ref,
                   q_ref, k_ref, v_ref, *, heads):
    p = p_ref[...]
    cos, sa, sb = cos_ref[...], sa_ref[...], sb_ref[...]

    def rms(x, g):
        return x * lax.rsqrt(jnp.mean(x * x, -1, keepdims=True) + RMS_EPS) * g

    q_lat = rms(p[:, :MLA_Q_LORA], qg_ref[...]).astype(BF16)
    kv_lat = rms(p[:, MLA_Q_LORA:MLA_Q_LORA + MLA_KV_LORA], kvg_ref[...]).astype(BF16)
    k_pe = p[:, MLA_Q_LORA + MLA_KV_LORA:]
    k_pe = jnp.concatenate([k_pe, jnp.zeros((k_pe.shape[0], LANES - MLA_ROPE), F32)], axis=1)
    k_pe = _rope(k_pe, cos, sa, sb).astype(BF16)

    q = _dot(q_lat, wq_ref[...]) * MLA_SCALE
    k_nope = _dot(kv_lat, wk_ref[...])
    v_ref[...] = _dot(kv_lat, wv_ref[...]).astype(BF16)
    for h in range(heads):
        base = h * MLA_QK_PAD
        q_ref[:, base:base + MLA_NOPE] = q[:, base:base + MLA_NOPE].astype(BF16)
        q_ref[:, base + MLA_NOPE:base + MLA_QK_PAD] = _rope(
            q[:, base + MLA_NOPE:base + MLA_QK_PAD], cos, sa, sb).astype(BF16)
        k_ref[:, base:base + MLA_NOPE] = k_nope[:, h * MLA_NOPE:(h + 1) * MLA_NOPE].astype(BF16)
        k_ref[:, base + MLA_NOPE:base + MLA_QK_PAD] = k_pe


def _mla_up(p_mla, tables, q_norm_g, w_uq, kv_norm_g, w_ukv, *, seq):
    t, cols = p_mla.shape
    heads = w_uq.shape[1] // (MLA_NOPE + MLA_ROPE)
    tm = min(TOKEN_TILE, seq)
    wq = w_uq.reshape(MLA_Q_LORA, heads, MLA_NOPE + MLA_ROPE)
    wq = jnp.pad(wq, ((0, 0), (0, 0), (0, MLA_QK_PAD - MLA_NOPE - MLA_ROPE)))
    wq = wq.reshape(MLA_Q_LORA, heads * MLA_QK_PAD).astype(BF16)
    wkv = w_ukv.reshape(MLA_KV_LORA, heads, MLA_NOPE + MLA_V)
    wk = wkv[:, :, :MLA_NOPE].reshape(MLA_KV_LORA, heads * MLA_NOPE).astype(BF16)
    wv = wkv[:, :, MLA_NOPE:].reshape(MLA_KV_LORA, heads * MLA_V).astype(BF16)
    tok = lambda n: pl.BlockSpec((tm, n), lambda i: (i, 0))
    full = lambda a: pl.BlockSpec(a.shape, lambda i: (0, 0))
    qg = q_norm_g.reshape(1, -1)
    kvg = kv_norm_g.reshape(1, -1)
    kern = functools.partial(_mla_up_kernel, heads=heads)
    return pl.pallas_call(
        kern,
        grid=(t // tm,),
        in_specs=[tok(cols), tok(LANES), tok(LANES), tok(LANES),
                  full(qg), full(wq), full(kvg), full(wk), full(wv)],
        out_specs=[tok(heads * MLA_QK_PAD), tok(heads * MLA_QK_PAD), tok(heads * MLA_V)],
        out_shape=[jax.ShapeDtypeStruct((t, heads * MLA_QK_PAD), BF16),
                   jax.ShapeDtypeStruct((t, heads * MLA_QK_PAD), BF16),
                   jax.ShapeDtypeStruct((t, heads * MLA_V), BF16)],
        compiler_params=_cparams("parallel"),
        name="mla_up",
    )(p_mla, *tables, qg, wq, kvg, wk, wv)


def _attn_kernel(q_ref, k_ref, v_ref, o_ref, m_ref, l_ref, acc_ref, *, tq):
    i = pl.program_id(2)
    j = pl.program_id(3)

    @pl.when(j == 0)
    def _():
        m_ref[...] = jnp.full_like(m_ref, -jnp.inf)
        l_ref[...] = jnp.zeros_like(l_ref)
        acc_ref[...] = jnp.zeros_like(acc_ref)

    def step(masked):
        s = _dot_nt(q_ref[...], k_ref[...])
        if masked:
            row = lax.broadcasted_iota(jnp.int32, s.shape, 0)
            col = lax.broadcasted_iota(jnp.int32, s.shape, 1)
            s = jnp.where(row >= col, s, -jnp.inf)
        m_prev = m_ref[...]
        m_new = jnp.maximum(m_prev, jnp.max(s, -1, keepdims=True))
        alpha = jnp.exp(m_prev - m_new)
        p = jnp.exp(s - m_new)
        l_ref[...] = alpha * l_ref[...] + jnp.sum(p, -1, keepdims=True)
        acc_ref[...] = alpha * acc_ref[...] + _dot(p.astype(BF16), v_ref[...])
        m_ref[...] = m_new

    @pl.when(j < i)
    def _():
        step(False)

    @pl.when(j == i)
    def _():
        step(True)
        o_ref[...] = (acc_ref[...] / l_ref[...]).astype(o_ref.dtype)


def _attention(q, k, v, *, batch, seq):
    t = q.shape[0]
    heads = v.shape[1] // MLA_V
    tq = min(ATTN_TILE, seq)
    nq = seq // tq
    kern = functools.partial(_attn_kernel, tq=tq)
    return pl.pallas_call(
        kern,
        grid=(batch, heads, nq, nq),
        in_specs=[pl.BlockSpec((tq, MLA_QK_PAD), lambda b, h, i, j: (b * nq + i, h)),
                  pl.BlockSpec((tq, MLA_QK_PAD),
                               lambda b, h, i, j: (b * nq + jnp.minimum(j, i), h)),
                  pl.BlockSpec((tq, MLA_V),
                               lambda b, h, i, j: (b * nq + jnp.minimum(j, i), h))],
        out_specs=pl.BlockSpec((tq, MLA_V), lambda b, h, i, j: (b * nq + i, h)),
        out_shape=jax.ShapeDtypeStruct((t, heads * MLA_V), BF16),
        scratch_shapes=[pltpu.VMEM((tq, 1), F32), pltpu.VMEM((tq, 1), F32),
                        pltpu.VMEM((tq, MLA_V), F32)],
        compiler_params=_cparams("parallel", "parallel", "parallel", "arbitrary"),
        name="attention",
    )(q, k, v)


def _conv_kernel(lin_ref, gate_ref, w_ref, b_ref, lng_ref, lnb_ref, o_ref, u_ref, *, tm):
    j = pl.program_id(1)

    @pl.when(j == 0)
    def _():
        u_ref[0:CONV_HALO, :] = jnp.zeros((CONV_HALO, u_ref.shape[1]), F32)

    u_ref[CONV_HALO:, :] = lin_ref[0] * _sigmoid(gate_ref[0])
    first = CONV_HALO - (CONV_K - 1)
    acc = jnp.zeros((tm, u_ref.shape[1]), F32)
    for tap in range(CONV_K):
        acc = acc + u_ref[first + tap:first + tap + tm, :] * w_ref[tap:tap + 1, :]
    halo = u_ref[tm:tm + CONV_HALO, :]
    u_ref[0:CONV_HALO, :] = halo
    y = _layer_norm(acc + b_ref[...], lng_ref[...], lnb_ref[...])
    o_ref[0] = _silu(y).astype(o_ref.dtype)


def _conv(p_conv, conv_w, conv_b, ln_g, ln_b, *, batch, seq):
    ch = conv_w.shape[1]
    tm = min(TOKEN_TILE, seq)
    p3 = p_conv.reshape(batch, seq, 2 * ch)
    full = lambda a: pl.BlockSpec(a.shape, lambda b, j: (0, 0))
    row = lambda a: a.reshape(1, -1)
    kern = functools.partial(_conv_kernel, tm=tm)
    out = pl.pallas_call(
        kern,
        grid=(batch, seq // tm),
        in_specs=[pl.BlockSpec((1, tm, ch), lambda b, j: (b, j, 0)),
                  pl.BlockSpec((1, tm, ch), lambda b, j: (b, j, 1)),
                  full(conv_w), full(row(conv_b)), full(row(ln_g)), full(row(ln_b))],
        out_specs=pl.BlockSpec((1, tm, ch), lambda b, j: (b, j, 0)),
        out_shape=jax.ShapeDtypeStruct((batch, seq, ch), BF16),
        scratch_shapes=[pltpu.VMEM((CONV_HALO + tm, ch), F32)],
        compiler_params=_cparams("parallel", "arbitrary"),
        name="conv",
    )(p3, p3, conv_w, row(conv_b), row(ln_g), row(ln_b))
    return out.reshape(batch * seq, ch)


def _out_proj_kernel(x_ref, mod_ref, y1_ref, y2_ref, y3_ref, w1_ref, w2_ref, w3_ref,
                     lng_ref, lnb_ref, o_ref, *, mod_base, alpha):
    y = (_dot(y1_ref[...], w1_ref[...]) + _dot(y2_ref[...], w2_ref[...])
         + _dot(y3_ref[...], w3_ref[...]))
    g = mod_ref[0, mod_base + 2:mod_base + 3, :]
    o_ref[...] = _layer_norm(alpha * x_ref[...] + (1.0 + g) * y, lng_ref[...], lnb_ref[...])


def _out_proj(x, mod, ys, ws, ln_g, ln_b, *, mod_base, alpha, seq):
    t, d = x.shape
    tm = min(TOKEN_TILE, seq)
    per_b = seq // tm
    tok = lambda n: pl.BlockSpec((tm, n), lambda i: (i, 0))
    full = lambda a: pl.BlockSpec(a.shape, lambda i: (0, 0))
    kern = functools.partial(_out_proj_kernel, mod_base=mod_base, alpha=alpha)
    return pl.pallas_call(
        kern,
        grid=(t // tm,),
        in_specs=[tok(d), pl.BlockSpec((1, N_MOD, d), lambda i: (i // per_b, 0, 0)),
                  *[tok(y.shape[1]) for y in ys], *[full(w) for w in ws],
                  full(ln_g), full(ln_b)],
        out_specs=tok(d),
        out_shape=jax.ShapeDtypeStruct((t, d), F32),
        compiler_params=_cparams("parallel"),
        name="out_proj",
    )(x, mod, *ys, *ws, ln_g, ln_b)


def kernel(x, c, positions, w_ada, b_ada, ln_g, ln_b, w_ffn1_in, w_ffn1_out, w_ffn2_in, w_ffn2_out, w_in, w_out, rw_mu, rw_w0, rw_w2, rw_a0, rw_a2, rw_g2, rw_k_k, rw_k_a, rw_r_k, rw_lnx_g, rw_lnx_b, mla_q_norm_g, mla_w_uq, mla_kv_norm_g, mla_w_ukv, conv_w, conv_b, conv_ln_g, conv_ln_b):
    batch, seq, d = x.shape
    depth = w_ada.shape[0]
    alpha = (2 * depth) ** 0.25
    rw_width = rw_w0.shape[1]
    rw_cols = rw_mu.shape[1]
    mla_cols = MLA_Q_LORA + MLA_KV_LORA + MLA_ROPE
    mla_width = (mla_w_ukv.shape[2] // (MLA_NOPE + MLA_V)) * MLA_V

    mod_all = _ada(c, w_ada, b_ada)[:, :batch].reshape(depth, batch, N_MOD, d)
    tables = _rope_tables(positions)
    xt = x.reshape(batch * seq, d)
    row = lambda a: a.reshape(1, -1)

    for l in range(depth):
        mod = mod_all[l]
        xt = _ffn(xt, mod, w_ffn1_in[l].astype(BF16), w_ffn1_out[l].astype(BF16),
                  row(ln_g[l, 0]), row(ln_b[l, 0]), mod_base=0, alpha=alpha, seq=seq)

        w_in_l = w_in[l].astype(BF16)
        p_rw = _mod_matmul(xt, mod, w_in_l[:, :rw_cols], mod_base=3, seq=seq, tn=rw_cols // 2)
        p_mla = _mod_matmul(xt, mod, w_in_l[:, rw_cols:rw_cols + mla_cols],
                            mod_base=3, seq=seq, tn=mla_cols)
        p_conv = _mod_matmul(xt, mod, w_in_l[:, rw_cols + mla_cols:],
                             mod_base=3, seq=seq, tn=w_in.shape[2] - rw_cols - mla_cols)

        y_rw = _rwkv(p_rw, rw_mu[l], rw_w0[l], rw_w2[l], rw_a0[l], rw_a2[l], rw_g2[l],
                     rw_k_k[l], rw_k_a[l], rw_r_k[l], rw_lnx_g[l], rw_lnx_b[l],
                     batch=batch, seq=seq)
        q, k, v = _mla_up(p_mla, tables, mla_q_norm_g[l], mla_w_uq[l], mla_kv_norm_g[l],
                          mla_w_ukv[l], seq=seq)
        y_mla = _attention(q, k, v, batch=batch, seq=seq)
        y_conv = _conv(p_conv, conv_w[l], conv_b[l], conv_ln_g[l], conv_ln_b[l],
                       batch=batch, seq=seq)

        w_out_l = w_out[l].astype(BF16)
        ws = (w_out_l[:rw_width], w_out_l[rw_width:rw_width + mla_width],
              w_out_l[rw_width + mla_width:])
        xt = _out_proj(xt, mod, (y_rw, y_mla, y_conv), ws, row(ln_g[l, 1]), row(ln_b[l, 1]),
                       mod_base=3, alpha=alpha, seq=seq)

        xt = _ffn(xt, mod, w_ffn2_in[l].astype(BF16), w_ffn2_out[l].astype(BF16),
                  row(ln_g[l, 2]), row(ln_b[l, 2]), mod_base=6, alpha=alpha, seq=seq)
    return xt.reshape(batch, seq, d)
```

```python
import functools

import jax
import jax.numpy as jnp
from jax import lax
from jax.experimental import pallas as pl
from jax.experimental.pallas import tpu as pltpu

F32 = jnp.float32
BF16 = jnp.bfloat16
HIGHEST = lax.Precision.HIGHEST

LANES = 128
SUBLANES = 8
RW_HEAD_DIM = 64
RW_CHUNK = 64
RW_INV_BASE = 16
RW_OUT_DTYPE = jnp.bfloat16
RW_DECAY_LORA = 64
RW_AAA_LORA = 64
RW_GATE_LORA = 128
RW_GN_EPS = 64e-5
MLA_NOPE = 128
MLA_ROPE = 64
MLA_V = 128
MLA_Q_LORA = 384
MLA_KV_LORA = 256
MLA_QK_PAD = 256
MLA_SCALE = (MLA_NOPE + MLA_ROPE) ** -0.5
ROPE_THETA = 10000.0
CONV_K = 31
CONV_HALO = 32
N_MOD = 9
FFN_RES = 0.5
LN_EPS = 1e-5
RMS_EPS = 1e-6
VMEM_LIMIT = 52 * 1024 * 1024
TOKEN_TILE = 512
ATTN_TILE = 512
RW_BLOCK = 256
FFN_TILE = 512


def _cparams(*sem):
    return pltpu.CompilerParams(dimension_semantics=sem, vmem_limit_bytes=VMEM_LIMIT)


def _dot(a, b):
    return jnp.dot(a, b, preferred_element_type=F32)


def _dot_hi(a, b):
    return jnp.dot(a, b, preferred_element_type=F32, precision=HIGHEST)


def _dot_nt(a, b):
    return lax.dot_general(a, b, (((1,), (1,)), ((), ())), preferred_element_type=F32)


def _rw_dot(a, b):
    return _dot(a.astype(BF16), b.astype(BF16))


def _split_bf16(x, terms):
    parts = []
    for _ in range(terms):
        hi = x.astype(BF16)
        parts.append(hi)
        x = x - hi.astype(F32)
    return parts


def _dot_split_lhs(x, exact_rhs, terms):
    return sum(_dot(part, exact_rhs) for part in _split_bf16(x, terms))


def _dot_split_rhs(exact_lhs, x, terms):
    return sum(_dot(exact_lhs, part) for part in _split_bf16(x, terms))


def _mm3(a, b):
    a_hi, a_lo = _split_bf16(a, 2)
    b_hi, b_lo = _split_bf16(b, 2)
    k, n = b.shape
    if k != LANES:
        return _dot(a_hi, b_hi) + _dot(a_lo, b_hi) + _dot(a_hi, b_lo)
    lhs = jnp.concatenate([a_hi, a_lo], axis=1)
    if n != LANES:
        return _dot(lhs, jnp.concatenate([b_hi, b_hi], axis=0)) + _dot(a_hi, b_lo)
    rhs = jnp.concatenate([jnp.concatenate([b_hi, b_lo], axis=1),
                           jnp.concatenate([b_hi, jnp.zeros_like(b_lo)], axis=1)], axis=0)
    out = _dot(lhs, rhs)
    return out[:, :n] + out[:, n:]


def _mm3_nt(a, b):
    a_hi, a_lo = _split_bf16(a, 2)
    b_hi, b_lo = _split_bf16(b, 2)
    lhs = jnp.concatenate([a_hi, a_lo], axis=1)
    rhs = jnp.concatenate([jnp.concatenate([b_hi, b_hi], axis=1),
                           jnp.concatenate([b_lo, jnp.zeros_like(b_lo)], axis=1)], axis=0)
    out = _dot_nt(lhs, rhs)
    n = b.shape[0]
    return out[:, :n] + out[:, n:]


def _sigmoid(x):
    return 1.0 / (1.0 + jnp.exp(-x))


def _silu(x):
    return x * _sigmoid(x)


def _layer_norm(y, g, b):
    mean = jnp.mean(y, -1, keepdims=True)
    d = y - mean
    var = jnp.mean(d * d, -1, keepdims=True)
    return d * lax.rsqrt(var + LN_EPS) * g + b


def _ada_kernel(c_ref, w_ref, b_ref, o_ref):
    o_ref[0] = _dot_hi(_silu(c_ref[...]), w_ref[0]) + b_ref[0]


def _ada(c, w_ada, b_ada):
    depth, d, n = w_ada.shape
    b = SUBLANES * pl.cdiv(c.shape[0], SUBLANES)
    c = jnp.pad(c, ((0, b - c.shape[0]), (0, 0)))
    tn = 1024
    return pl.pallas_call(
        _ada_kernel,
        grid=(depth, n // tn),
        in_specs=[pl.BlockSpec((b, d), lambda l, j: (0, 0)),
                  pl.BlockSpec((1, d, tn), lambda l, j: (l, 0, j)),
                  pl.BlockSpec((1, 1, tn), lambda l, j: (l, 0, j))],
        out_specs=pl.BlockSpec((1, b, tn), lambda l, j: (l, 0, j)),
        out_shape=jax.ShapeDtypeStruct((depth, b, n), F32),
        compiler_params=_cparams("parallel", "parallel"),
        name="ada",
    )(c, w_ada, b_ada.reshape(depth, 1, n))


def _ffn_kernel(x_ref, mod_ref, wg_ref, wu_ref, wo_ref, lng_ref, lnb_ref, o_ref,
                h_ref, acc_ref, *, mod_base, alpha):
    j = pl.program_id(1)

    @pl.when(j == 0)
    def _():
        sh = mod_ref[0, mod_base:mod_base + 1, :]
        sc = mod_ref[0, mod_base + 1:mod_base + 2, :]
        h_ref[...] = (x_ref[...] * (1.0 + sc) + sh).astype(BF16)
        acc_ref[...] = jnp.zeros_like(acc_ref)

    h = h_ref[...]
    gate = _dot(h, wg_ref[...])
    up = _dot(h, wu_ref[...])
    act = (_silu(gate) * up).astype(BF16)
    acc_ref[...] += _dot(act, wo_ref[...])

    @pl.when(j == pl.num_programs(1) - 1)
    def _():
        g = mod_ref[0, mod_base + 2:mod_base + 3, :]
        y = alpha * x_ref[...] + (FFN_RES * (1.0 + g)) * acc_ref[...]
        o_ref[...] = _layer_norm(y, lng_ref[...], lnb_ref[...])


def _ffn(x, mod, w_in, w_out, ln_g, ln_b, *, mod_base, alpha, seq):
    t, d = x.shape
    f = w_out.shape[0]
    tm = min(TOKEN_TILE, seq)
    tf = FFN_TILE
    nf = f // tf
    per_b = seq // tm
    kern = functools.partial(_ffn_kernel, mod_base=mod_base, alpha=alpha)
    return pl.pallas_call(
        kern,
        grid=(t // tm, nf),
        in_specs=[pl.BlockSpec((tm, d), lambda i, j: (i, 0)),
                  pl.BlockSpec((1, N_MOD, d), lambda i, j: (i // per_b, 0, 0)),
                  pl.BlockSpec((d, tf), lambda i, j: (0, j)),
                  pl.BlockSpec((d, tf), lambda i, j: (0, nf + j)),
                  pl.BlockSpec((tf, d), lambda i, j: (j, 0)),
                  pl.BlockSpec((1, d), lambda i, j: (0, 0)),
                  pl.BlockSpec((1, d), lambda i, j: (0, 0))],
        out_specs=pl.BlockSpec((tm, d), lambda i, j: (i, 0)),
        out_shape=jax.ShapeDtypeStruct((t, d), F32),
        scratch_shapes=[pltpu.VMEM((tm, d), BF16), pltpu.VMEM((tm, d), F32)],
        compiler_params=_cparams("parallel", "arbitrary"),
        name="ffn",
    )(x, mod, w_in, w_in, w_out, ln_g, ln_b)


def _mod_matmul_kernel(x_ref, mod_ref, w_ref, o_ref, *, mod_base):
    sh = mod_ref[0, mod_base:mod_base + 1, :]
    sc = mod_ref[0, mod_base + 1:mod_base + 2, :]
    h = (x_ref[...] * (1.0 + sc) + sh).astype(BF16)
    o_ref[...] = _dot(h, w_ref[...])


def _mod_matmul(x, mod, w, *, mod_base, seq, tn):
    t, d = x.shape
    n = w.shape[1]
    tm = min(TOKEN_TILE, seq)
    per_b = seq // tm
    kern = functools.partial(_mod_matmul_kernel, mod_base=mod_base)
    return pl.pallas_call(
        kern,
        grid=(n // tn, t // tm),
        in_specs=[pl.BlockSpec((tm, d), lambda j, i: (i, 0)),
                  pl.BlockSpec((1, N_MOD, d), lambda j, i: (i // per_b, 0, 0)),
                  pl.BlockSpec((d, tn), lambda j, i: (0, j))],
        out_specs=pl.BlockSpec((tm, tn), lambda j, i: (i, j)),
        out_shape=jax.ShapeDtypeStruct((t, n), F32),
        compiler_params=_cparams("parallel", "parallel"),
        name="in_proj",
    )(x, mod, w)


def _rwkv_kernel(r_ref, k_ref, v_ref, lo_ref, mur_ref, muk_ref, muv_ref, mulo_ref,
                 w0_ref, w2_ref, a0_ref, a2_ref, g2_ref, kk_ref, ka_ref, rk_ref,
                 lng_ref, lnb_ref, o_ref,
                 state_ref, pr_ref, pk_ref, pv_ref, plo_ref, obuf_ref, *, bb, tb):
    j = pl.program_id(1)
    L = RW_CHUNK
    nc = tb // L

    @pl.when(j == 0)
    def _():
        state_ref[...] = jnp.zeros_like(state_ref)
        pr_ref[...] = jnp.zeros_like(pr_ref)
        pk_ref[...] = jnp.zeros_like(pk_ref)
        pv_ref[...] = jnp.zeros_like(pv_ref)
        plo_ref[...] = jnp.zeros_like(plo_ref)

    lane_r = lax.broadcasted_iota(jnp.int32, (LANES, LANES), 0)
    lane_c = lax.broadcasted_iota(jnp.int32, (LANES, LANES), 1)

    def same_block(size):
        return (lane_r // size) == (lane_c // size)

    same_head = same_block(RW_HEAD_DIM).astype(BF16)
    stack_mask = ((lane_r // L) == (lane_c // RW_HEAD_DIM)).astype(F32)
    strict_lower = lane_r > lane_c
    lower = lane_r >= lane_c
    eye = (lane_r == lane_c).astype(F32)
    tri = (lax.broadcasted_iota(jnp.int32, (L, L), 0)
           >= lax.broadcasted_iota(jnp.int32, (L, L), 1)).astype(BF16)
    zeros = jnp.zeros((LANES, LANES), F32)

    def head_sum(x):
        return _dot_split_lhs(x, same_head, 2)

    def stack(x):
        return jnp.concatenate([x, x], axis=0) * stack_mask

    def prologue(b):
        def shift_mix(p_ref, prev_ref, mu_ref):
            p = p_ref[b]
            row = lax.broadcasted_iota(jnp.int32, p.shape, 0)
            prev = jnp.where(row == 0, prev_ref[b, 0:1, :], pltpu.roll(p, 1, axis=0))
            prev_ref[b, 0:1, :] = p[tb - 1:tb, :]
            return p + (prev - p) * mu_ref[...]

        r = shift_mix(r_ref, pr_ref, mur_ref)
        k = shift_mix(k_ref, pk_ref, muk_ref)
        v = shift_mix(v_ref, pv_ref, muv_ref)
        lo = shift_mix(lo_ref, plo_ref, mulo_ref)
        w_lo = lo[:, :RW_DECAY_LORA]
        a_lo = lo[:, RW_DECAY_LORA:RW_DECAY_LORA + RW_AAA_LORA]
        g_lo = lo[:, RW_DECAY_LORA + RW_AAA_LORA:]
        z = w0_ref[...] + _rw_dot(jnp.tanh(w_lo), w2_ref[...])
        softplus_neg_z = jnp.maximum(-z, 0.0) + jnp.log(1.0 + jnp.exp(-jnp.abs(z)))
        log_decay = -jnp.exp(-softplus_neg_z - 0.5)
        a = _sigmoid(a0_ref[...] + _rw_dot(a_lo, a2_ref[...]))
        gate = _rw_dot(_sigmoid(g_lo), g2_ref[...])
        kk = k * kk_ref[...]
        kk = kk / jnp.maximum(jnp.sqrt(head_sum(kk * kk)), 1e-12)
        k = k * (1.0 + (a - 1.0) * ka_ref[...])
        return r, k, v, kk, a, gate, log_decay

    seqs = [prologue(b) for b in range(bb)]
    items = [(b, c) for c in range(nc) for b in range(bb)]
    every = range(len(items))

    def chunk_operands(b, c):
        r, k, v, kk, a, _, log_decay = seqs[b]
        rows = slice(c * L, (c + 1) * L)
        lw = log_decay[rows]
        cum = _dot_split_rhs(tri, lw, 3)
        w_cum = jnp.exp(cum)
        w_inv = jnp.exp(-cum)
        w_prev = jnp.exp(cum - lw)
        w_last = w_cum[L - 1:L, :]
        kk_c = kk[rows]
        return (stack(-kk_c * w_prev), stack(kk_c * a[rows] * w_inv), stack(k[rows] * w_inv),
                stack(r[rows] * w_cum), stack(v[rows]), w_last)

    ops = [chunk_operands(b, c) for b, c in items]
    a2_ = [o[0] for o in ops]
    b2_ = [o[1] for o in ops]
    k2_ = [o[2] for o in ops]
    r2_ = [o[3] for o in ops]
    v2_ = [o[4] for o in ops]
    w_last = [o[5] for o in ops]

    scores = [_mm3_nt(jnp.concatenate([a2_[i], r2_[i]], axis=0),
                      jnp.concatenate([b2_[i], k2_[i]], axis=0)) for i in every]
    a_ab = [jnp.where(strict_lower, s[:LANES, :LANES], 0.0) for s in scores]
    a_ak = [jnp.where(strict_lower, s[:LANES, LANES:], 0.0) for s in scores]
    a_rb = [jnp.where(lower, s[LANES:, :LANES], 0.0) for s in scores]
    a_rk = [jnp.where(lower, s[LANES:, LANES:], 0.0) for s in scores]

    diag_blocks = same_block(RW_INV_BASE)
    p = [jnp.where(diag_blocks, x, 0.0) for x in a_ab]
    t = [eye + x for x in p]
    n = 2
    while n < RW_INV_BASE:
        p = [_mm3(x, x) for x in p]
        t = [t[i] + _mm3(t[i], p[i]) for i in every]
        n *= 2
    size = 2 * RW_INV_BASE
    while size <= L:
        off_diag = same_block(size) & ~same_block(size // 2)
        et = [_mm3(jnp.where(off_diag, a_ab[i], 0.0), t[i]) for i in every]
        t = [t[i] + _mm3(t[i], et[i]) for i in every]
        size *= 2

    akv = [_mm3(a_ak[i], v2_[i]) for i in every]
    x = [_mm3(t[i], jnp.concatenate([a2_[i], akv[i]], axis=1)) for i in every]
    big = []
    for i in every:
        lhs = jnp.concatenate(
            [jnp.concatenate([a_rb[i], a_rk[i]], axis=1),
             jnp.concatenate([(b2_[i] * w_last[i]).T, (k2_[i] * w_last[i]).T], axis=1)], axis=0)
        rhs = jnp.concatenate([x[i], jnp.concatenate([zeros, v2_[i]], axis=1)], axis=0)
        big.append(_mm3(lhs, rhs))

    states = [state_ref[b] for b in range(bb)]
    for i, (b, c) in enumerate(items):
        r_hat = r2_[i] + big[i][:LANES, :LANES]
        st = _mm3(jnp.concatenate([r_hat, big[i][LANES:, :LANES]], axis=0), states[b])
        o2 = st[:LANES] + big[i][:LANES, LANES:]
        obuf_ref[b, c * L:(c + 1) * L, :] = o2[:L] + o2[L:]
        w_rows = jnp.broadcast_to(w_last[i], (LANES, LANES)).T
        states[b] = states[b] * w_rows + st[LANES:] + big[i][LANES:, LANES:]
    for b in range(bb):
        state_ref[b] = states[b]

    inv_n = 1.0 / RW_HEAD_DIM
    for b in range(bb):
        r, k, v, _, _, gate, _ = seqs[b]
        o = obuf_ref[b]
        mean = head_sum(o) * inv_n
        d = o - mean
        var = head_sum(d * d) * inv_n
        on = d * lax.rsqrt(var + RW_GN_EPS) * lng_ref[...] + lnb_ref[...]
        bonus = head_sum(r * k * rk_ref[...]) * v
        o_ref[b] = ((on + bonus) * gate).astype(o_ref.dtype)


def _rwkv(p_rw, mu, w0, w2, a0, a2, g2, k_k, k_a, r_k, lnx_g, lnx_b, *, batch, seq):
    width = w0.shape[-1]
    pairs = width // LANES
    tb = min(RW_BLOCK, seq)
    lo_w = RW_DECAY_LORA + RW_AAA_LORA + RW_GATE_LORA
    lo_blk = 3 * width // lo_w
    p3 = p_rw.reshape(batch, seq, p_rw.shape[-1])

    def col(off):
        return lambda h, j: (0, j, off + h)

    def vec(off):
        return lambda h, j: (0, off + h)

    row = lambda a: a.reshape(1, -1)
    kern = functools.partial(_rwkv_kernel, bb=batch, tb=tb)
    out = pl.pallas_call(
        kern,
        grid=(pairs, seq // tb),
        in_specs=[pl.BlockSpec((batch, tb, LANES), col(0)),
                  pl.BlockSpec((batch, tb, LANES), col(pairs)),
                  pl.BlockSpec((batch, tb, LANES), col(2 * pairs)),
                  pl.BlockSpec((batch, tb, lo_w), lambda h, j: (0, j, lo_blk)),
                  pl.BlockSpec((1, LANES), vec(0)),
                  pl.BlockSpec((1, LANES), vec(pairs)),
                  pl.BlockSpec((1, LANES), vec(2 * pairs)),
                  pl.BlockSpec((1, lo_w), lambda h, j: (0, lo_blk)),
                  pl.BlockSpec((1, LANES), vec(0)),
                  pl.BlockSpec((RW_DECAY_LORA, LANES), vec(0)),
                  pl.BlockSpec((1, LANES), vec(0)),
                  pl.BlockSpec((RW_AAA_LORA, LANES), vec(0)),
                  pl.BlockSpec((RW_GATE_LORA, LANES), vec(0)),
                  pl.BlockSpec((1, LANES), vec(0)),
                  pl.BlockSpec((1, LANES), vec(0)),
                  pl.BlockSpec((1, LANES), vec(0)),
                  pl.BlockSpec((1, LANES), vec(0)),
                  pl.BlockSpec((1, LANES), vec(0))],
        out_specs=pl.BlockSpec((batch, tb, LANES), lambda h, j: (0, j, h)),
        out_shape=jax.ShapeDtypeStruct((batch, seq, width), RW_OUT_DTYPE),
        scratch_shapes=[pltpu.VMEM((batch, LANES, LANES), F32),
                        pltpu.VMEM((batch, SUBLANES, LANES), F32),
                        pltpu.VMEM((batch, SUBLANES, LANES), F32),
                        pltpu.VMEM((batch, SUBLANES, LANES), F32),
                        pltpu.VMEM((batch, SUBLANES, lo_w), F32),
                        pltpu.VMEM((batch, tb, LANES), F32)],
        compiler_params=_cparams("parallel", "arbitrary"),
        name="rwkv",
    )(p3, p3, p3, p3, row(mu), row(mu), row(mu), row(mu), row(w0), w2, row(a0), a2, g2,
      row(k_k), row(k_a), row(r_k), row(lnx_g), row(lnx_b))
    return out.reshape(batch * seq, width)


def _rope_table_kernel(pos_ref, freq_ref, cos_ref, sa_ref, sb_ref):
    half = MLA_ROPE // 2
    ang = pos_ref[...].astype(F32) * freq_ref[...]
    lane = lax.broadcasted_iota(jnp.int32, ang.shape, 1)
    cos = jnp.cos(ang)
    sin = jnp.sin(ang)
    cos_ref[...] = jnp.where(lane < MLA_ROPE, cos, 1.0)
    sa_ref[...] = jnp.where(lane < half, -sin, 0.0)
    sb_ref[...] = jnp.where((lane >= half) & (lane < MLA_ROPE), sin, 0.0)


def _rope_tables(positions):
    t = positions.size
    half = MLA_ROPE // 2
    tm = min(2048, t)
    inv_freq = ROPE_THETA ** (-jnp.arange(half, dtype=F32) / half)
    freq = jnp.concatenate([inv_freq, inv_freq, jnp.zeros((LANES - MLA_ROPE,), F32)])
    spec = pl.BlockSpec((tm, LANES), lambda i: (i, 0))
    shp = jax.ShapeDtypeStruct((t, LANES), F32)
    return pl.pallas_call(
        _rope_table_kernel,
        grid=(t // tm,),
        in_specs=[pl.BlockSpec((tm, 1), lambda i: (i, 0)),
                  pl.BlockSpec((1, LANES), lambda i: (0, 0))],
        out_specs=[spec, spec, spec],
        out_shape=[shp, shp, shp],
        compiler_params=_cparams("parallel"),
        name="rope_tables",
    )(positions.reshape(t, 1), freq.reshape(1, LANES))


def _rope(x, cos, sa, sb):
    half = MLA_ROPE // 2
    return x * cos + pltpu.roll(x, LANES - half, axis=1) * sa + pltpu.roll(x, half, axis=1) * sb


def _mla_up_kernel(p_ref, cos_ref, sa_ref, sb_ref, qg_ref, wq_ref, kvg_ref, wk_ref, wv_ref,
                   q_ref, k_ref, v_ref, *, heads):
    p = p_ref[...]
    cos, sa, sb = cos_ref[...], sa_ref[...], sb_ref[...]

    def rms(x, g):
        return x * lax.rsqrt(jnp.mean(x * x, -1, keepdims=True) + RMS_EPS) * g

    q_lat = rms(p[:, :MLA_Q_LORA], qg_ref[...]).astype(BF16)
    kv_lat = rms(p[:, MLA_Q_LORA:MLA_Q_LORA + MLA_KV_LORA], kvg_ref[...]).astype(BF16)
    k_pe = p[:, MLA_Q_LORA + MLA_KV_LORA:]
    k_pe = jnp.concatenate([k_pe, jnp.zeros((k_pe.shape[0], LANES - MLA_ROPE), F32)], axis=1)
    k_pe = _rope(k_pe, cos, sa, sb).astype(BF16)

    q = _dot(q_lat, wq_ref[...]) * MLA_SCALE
    k_nope = _dot(kv_lat, wk_ref[...])
    v_ref[...] = _dot(kv_lat, wv_ref[...]).astype(BF16)
    for h in range(heads):
        base = h * MLA_QK_PAD
        q_ref[:, base:base + MLA_NOPE] = q[:, base:base + MLA_NOPE].astype(BF16)
        q_ref[:, base + MLA_NOPE:base + MLA_QK_PAD] = _rope(
            q[:, base + MLA_NOPE:base + MLA_QK_PAD], cos, sa, sb).astype(BF16)
        k_ref[:, base:base + MLA_NOPE] = k_nope[:, h * MLA_NOPE:(h + 1) * MLA_NOPE].astype(BF16)
        k_ref[:, base + MLA_NOPE:base + MLA_QK_PAD] = k_pe


def _mla_up(p_mla, tables, q_norm_g, w_uq, kv_norm_g, w_ukv, *, seq):
    t, cols = p_mla.shape
    heads = w_uq.shape[1] // (MLA_NOPE + MLA_ROPE)
    tm = min(TOKEN_TILE, seq)
    wq = w_uq.reshape(MLA_Q_LORA, heads, MLA_NOPE + MLA_ROPE)
    wq = jnp.pad(wq, ((0, 0), (0, 0), (0, MLA_QK_PAD - MLA_NOPE - MLA_ROPE)))
    wq = wq.reshape(MLA_Q_LORA, heads * MLA_QK_PAD).astype(BF16)
    wkv = w_ukv.reshape(MLA_KV_LORA, heads, MLA_NOPE + MLA_V)
    wk = wkv[:, :, :MLA_NOPE].reshape(MLA_KV_LORA, heads * MLA_NOPE).astype(BF16)
    wv = wkv[:, :, MLA_NOPE:].reshape(MLA_KV_LORA, heads * MLA_V).astype(BF16)
    tok = lambda n: pl.BlockSpec((tm, n), lambda i: (i, 0))
    full = lambda a: pl.BlockSpec(a.shape, lambda i: (0, 0))
    qg = q_norm_g.reshape(1, -1)
    kvg = kv_norm_g.reshape(1, -1)
    kern = functools.partial(_mla_up_kernel, heads=heads)
    return pl.pallas_call(
        kern,
        grid=(t // tm,),
        in_specs=[tok(cols), tok(LANES), tok(LANES), tok(LANES),
                  full(qg), full(wq), full(kvg), full(wk), full(wv)],
        out_specs=[tok(heads * MLA_QK_PAD), tok(heads * MLA_QK_PAD), tok(heads * MLA_V)],
        out_shape=[jax.ShapeDtypeStruct((t, heads * MLA_QK_PAD), BF16),
                   jax.ShapeDtypeStruct((t, heads * MLA_QK_PAD), BF16),
                   jax.ShapeDtypeStruct((t, heads * MLA_V), BF16)],
        compiler_params=_cparams("parallel"),
        name="mla_up",
    )(p_mla, *tables, qg, wq, kvg, wk, wv)


def _attn_kernel(qi_ref, kj_ref, q_ref, k_ref, v_ref, o_ref, m_ref, l_ref, acc_ref, *, heads):
    step_id = pl.program_id(1)
    i = qi_ref[step_id]
    j = kj_ref[step_id]

    @pl.when(j == 0)
    def _():
        m_ref[...] = jnp.full_like(m_ref, -jnp.inf)
        l_ref[...] = jnp.zeros_like(l_ref)
        acc_ref[...] = jnp.zeros_like(acc_ref)

    def qk(h):
        return slice(h * MLA_QK_PAD, (h + 1) * MLA_QK_PAD)

    def vo(h):
        return slice(h * MLA_V, (h + 1) * MLA_V)

    def step(masked):
        hs = range(heads)
        s = [_dot_nt(q_ref[:, qk(h)], k_ref[:, qk(h)]) for h in hs]
        if masked:
            row = lax.broadcasted_iota(jnp.int32, s[0].shape, 0)
            col = lax.broadcasted_iota(jnp.int32, s[0].shape, 1)
            s = [jnp.where(row >= col, x, -jnp.inf) for x in s]
        m_prev = [m_ref[h] for h in hs]
        m_new = [jnp.maximum(m_prev[h], jnp.max(s[h], -1, keepdims=True)) for h in hs]
        alpha = [jnp.exp(m_prev[h] - m_new[h]) for h in hs]
        p = [jnp.exp(s[h] - m_new[h]) for h in hs]
        pv = [_dot(p[h].astype(BF16), v_ref[:, vo(h)]) for h in hs]
        for h in hs:
            l_ref[h] = alpha[h] * l_ref[h] + jnp.sum(p[h], -1, keepdims=True)
            acc_ref[:, vo(h)] = alpha[h] * acc_ref[:, vo(h)] + pv[h]
            m_ref[h] = m_new[h]

    @pl.when(j < i)
    def _():
        step(False)

    @pl.when(j == i)
    def _():
        step(True)
        for h in range(heads):
            o_ref[:, vo(h)] = (acc_ref[:, vo(h)] / l_ref[h]).astype(o_ref.dtype)


def _attention(q, k, v, *, batch, seq):
    t = q.shape[0]
    heads = v.shape[1] // MLA_V
    tq = min(ATTN_TILE, seq)
    nq = seq // tq
    pairs = [(i, j) for i in range(nq) for j in range(i + 1)]
    qi = jnp.asarray([p[0] for p in pairs], jnp.int32)
    kj = jnp.asarray([p[1] for p in pairs], jnp.int32)
    kern = functools.partial(_attn_kernel, heads=heads)
    grid_spec = pltpu.PrefetchScalarGridSpec(
        num_scalar_prefetch=2,
        grid=(batch, len(pairs)),
        in_specs=[pl.BlockSpec((tq, heads * MLA_QK_PAD), lambda b, s, qi, kj: (b * nq + qi[s], 0)),
                  pl.BlockSpec((tq, heads * MLA_QK_PAD), lambda b, s, qi, kj: (b * nq + kj[s], 0)),
                  pl.BlockSpec((tq, heads * MLA_V), lambda b, s, qi, kj: (b * nq + kj[s], 0))],
        out_specs=pl.BlockSpec((tq, heads * MLA_V), lambda b, s, qi, kj: (b * nq + qi[s], 0)),
        scratch_shapes=[pltpu.VMEM((heads, tq, 1), F32), pltpu.VMEM((heads, tq, 1), F32),
                        pltpu.VMEM((tq, heads * MLA_V), F32)])
    return pl.pallas_call(
        kern,
        grid_spec=grid_spec,
        out_shape=jax.ShapeDtypeStruct((t, heads * MLA_V), BF16),
        compiler_params=_cparams("parallel", "arbitrary"),
        name="attention",
    )(qi, kj, q, k, v)


def _conv_kernel(lin_ref, gate_ref, w_ref, b_ref, lng_ref, lnb_ref, o_ref, u_ref, *, tm):
    j = pl.program_id(1)

    @pl.when(j == 0)
    def _():
        u_ref[0:CONV_HALO, :] = jnp.zeros((CONV_HALO, u_ref.shape[1]), F32)

    u_ref[CONV_HALO:, :] = lin_ref[0] * _sigmoid(gate_ref[0])
    first = CONV_HALO - (CONV_K - 1)
    acc = jnp.zeros((tm, u_ref.shape[1]), F32)
    for tap in range(CONV_K):
        acc = acc + u_ref[first + tap:first + tap + tm, :] * w_ref[tap:tap + 1, :]
    halo = u_ref[tm:tm + CONV_HALO, :]
    u_ref[0:CONV_HALO, :] = halo
    y = _layer_norm(acc + b_ref[...], lng_ref[...], lnb_ref[...])
    o_ref[0] = _silu(y).astype(o_ref.dtype)


def _conv(p_conv, conv_w, conv_b, ln_g, ln_b, *, batch, seq):
    ch = conv_w.shape[1]
    tm = min(TOKEN_TILE, seq)
    p3 = p_conv.reshape(batch, seq, 2 * ch)
    full = lambda a: pl.BlockSpec(a.shape, lambda b, j: (0, 0))
    row = lambda a: a.reshape(1, -1)
    kern = functools.partial(_conv_kernel, tm=tm)
    out = pl.pallas_call(
        kern,
        grid=(batch, seq // tm),
        in_specs=[pl.BlockSpec((1, tm, ch), lambda b, j: (b, j, 0)),
                  pl.BlockSpec((1, tm, ch), lambda b, j: (b, j, 1)),
                  full(conv_w), full(row(conv_b)), full(row(ln_g)), full(row(ln_b))],
        out_specs=pl.BlockSpec((1, tm, ch), lambda b, j: (b, j, 0)),
        out_shape=jax.ShapeDtypeStruct((batch, seq, ch), BF16),
        scratch_shapes=[pltpu.VMEM((CONV_HALO + tm, ch), F32)],
        compiler_params=_cparams("parallel", "arbitrary"),
        name="conv",
    )(p3, p3, conv_w, row(conv_b), row(ln_g), row(ln_b))
    return out.reshape(batch * seq, ch)


def _out_proj_kernel(x_ref, mod_ref, y1_ref, y2_ref, y3_ref, w1_ref, w2_ref, w3_ref,
                     lng_ref, lnb_ref, o_ref, *, mod_base, alpha):
    y = (_dot(y1_ref[...], w1_ref[...]) + _dot(y2_ref[...], w2_ref[...])
         + _dot(y3_ref[...], w3_ref[...]))
    g = mod_ref[0, mod_base + 2:mod_base + 3, :]
    o_ref[...] = _layer_norm(alpha * x_ref[...] + (1.0 + g) * y, lng_ref[...], lnb_ref[...])


def _out_proj(x, mod, ys, ws, ln_g, ln_b, *, mod_base, alpha, seq):
    t, d = x.shape
    tm = min(TOKEN_TILE, seq)
    per_b = seq // tm
    tok = lambda n: pl.BlockSpec((tm, n), lambda i: (i, 0))
    full = lambda a: pl.BlockSpec(a.shape, lambda i: (0, 0))
    kern = functools.partial(_out_proj_kernel, mod_base=mod_base, alpha=alpha)
    return pl.pallas_call(
        kern,
        grid=(t // tm,),
        in_specs=[tok(d), pl.BlockSpec((1, N_MOD, d), lambda i: (i // per_b, 0, 0)),
                  *[tok(y.shape[1]) for y in ys], *[full(w) for w in ws],
                  full(ln_g), full(ln_b)],
        out_specs=tok(d),
        out_shape=jax.ShapeDtypeStruct((t, d), F32),
        compiler_params=_cparams("parallel"),
        name="out_proj",
    )(x, mod, *ys, *ws, ln_g, ln_b)


def kernel(x, c, positions, w_ada, b_ada, ln_g, ln_b, w_ffn1_in, w_ffn1_out, w_ffn2_in, w_ffn2_out, w_in, w_out, rw_mu, rw_w0, rw_w2, rw_a0, rw_a2, rw_g2, rw_k_k, rw_k_a, rw_r_k, rw_lnx_g, rw_lnx_b, mla_q_norm_g, mla_w_uq, mla_kv_norm_g, mla_w_ukv, conv_w, conv_b, conv_ln_g, conv_ln_b):
    batch, seq, d = x.shape
    depth = w_ada.shape[0]
    alpha = (2 * depth) ** 0.25
    rw_width = rw_w0.shape[1]
    rw_cols = rw_mu.shape[1]
    mla_cols = MLA_Q_LORA + MLA_KV_LORA + MLA_ROPE
    mla_width = (mla_w_ukv.shape[2] // (MLA_NOPE + MLA_V)) * MLA_V

    mod_all = _ada(c, w_ada, b_ada)[:, :batch].reshape(depth, batch, N_MOD, d)
    tables = _rope_tables(positions)
    xt = x.reshape(batch * seq, d)
    row = lambda a: a.reshape(1, -1)

    for l in range(depth):
        mod = mod_all[l]
        xt = _ffn(xt, mod, w_ffn1_in[l].astype(BF16), w_ffn1_out[l].astype(BF16),
                  row(ln_g[l, 0]), row(ln_b[l, 0]), mod_base=0, alpha=alpha, seq=seq)

        w_in_l = w_in[l].astype(BF16)
        p_rw = _mod_matmul(xt, mod, w_in_l[:, :rw_cols], mod_base=3, seq=seq, tn=rw_cols // 2)
        p_mla = _mod_matmul(xt, mod, w_in_l[:, rw_cols:rw_cols + mla_cols],
                            mod_base=3, seq=seq, tn=mla_cols)
        p_conv = _mod_matmul(xt, mod, w_in_l[:, rw_cols + mla_cols:],
                             mod_base=3, seq=seq, tn=w_in.shape[2] - rw_cols - mla_cols)

        y_rw = _rwkv(p_rw, rw_mu[l], rw_w0[l], rw_w2[l], rw_a0[l], rw_a2[l], rw_g2[l],
                     rw_k_k[l], rw_k_a[l], rw_r_k[l], rw_lnx_g[l], rw_lnx_b[l],
                     batch=batch, seq=seq)
        q, k, v = _mla_up(p_mla, tables, mla_q_norm_g[l], mla_w_uq[l], mla_kv_norm_g[l],
                          mla_w_ukv[l], seq=seq)
        y_mla = _attention(q, k, v, batch=batch, seq=seq)
        y_conv = _conv(p_conv, conv_w[l], conv_b[l], conv_ln_g[l], conv_ln_b[l],
                       batch=batch, seq=seq)

        w_out_l = w_out[l].astype(BF16)
        ws = (w_out_l[:rw_width], w_out_l[rw_width:rw_width + mla_width],
              w_out_l[rw_width + mla_width:])
        xt = _out_proj(xt, mod, (y_rw, y_mla, y_conv), ws, row(ln_g[l, 1]), row(ln_b[l, 1]),
                       mod_base=3, alpha=alpha, seq=seq)

        xt = _ffn(xt, mod, w_ffn2_in[l].astype(BF16), w_ffn2_out[l].astype(BF16),
                  row(ln_g[l, 2]), row(ln_b[l, 2]), mod_base=6, alpha=alpha, seq=seq)
    return xt.reshape(batch, seq, d)
```

```python
import functools

import jax
import jax.numpy as jnp
from jax import lax
from jax.experimental import pallas as pl
from jax.experimental.pallas import tpu as pltpu

F32 = jnp.float32
BF16 = jnp.bfloat16
HIGHEST = lax.Precision.HIGHEST

LANES = 128
SUBLANES = 8
RW_HEAD_DIM = 64
RW_CHUNK = 64
RW_INV_BASE = 16
RW_OUT_DTYPE = jnp.bfloat16
RW_DECAY_LORA = 64
RW_AAA_LORA = 64
RW_GATE_LORA = 128
RW_GN_EPS = 64e-5
MLA_NOPE = 128
MLA_ROPE = 64
MLA_V = 128
MLA_Q_LORA = 384
MLA_KV_LORA = 256
MLA_QK_PAD = 256
MLA_SCALE = (MLA_NOPE + MLA_ROPE) ** -0.5
ROPE_THETA = 10000.0
CONV_K = 31
CONV_HALO = 32
N_MOD = 9
FFN_RES = 0.5
LN_EPS = 1e-5
RMS_EPS = 1e-6
VMEM_LIMIT = 52 * 1024 * 1024
TOKEN_TILE = 512
ATTN_TILE = 512
RW_BLOCK = 512
FFN_TILE = 512


def _cparams(*sem):
    return pltpu.CompilerParams(dimension_semantics=sem, vmem_limit_bytes=VMEM_LIMIT)


def _dot(a, b):
    return jnp.dot(a, b, preferred_element_type=F32)


def _dot_hi(a, b):
    return jnp.dot(a, b, preferred_element_type=F32, precision=HIGHEST)


def _dot_nt(a, b):
    return lax.dot_general(a, b, (((1,), (1,)), ((), ())), preferred_element_type=F32)


def _rw_dot(a, b):
    return _dot(a.astype(BF16), b.astype(BF16))


def _split_bf16(x, terms):
    parts = []
    for _ in range(terms):
        hi = x.astype(BF16)
        parts.append(hi)
        x = x - hi.astype(F32)
    return parts


def _dot_split_lhs(x, exact_rhs, terms):
    return sum(_dot(part, exact_rhs) for part in _split_bf16(x, terms))


def _dot_split_rhs(exact_lhs, x, terms):
    return sum(_dot(exact_lhs, part) for part in _split_bf16(x, terms))


def _mm3(a, b):
    a_hi, a_lo = _split_bf16(a, 2)
    b_hi, b_lo = _split_bf16(b, 2)
    k, n = b.shape
    if k != LANES:
        return _dot(a_hi, b_hi) + _dot(a_lo, b_hi) + _dot(a_hi, b_lo)
    lhs = jnp.concatenate([a_hi, a_lo], axis=1)
    if n != LANES:
        return _dot(lhs, jnp.concatenate([b_hi, b_hi], axis=0)) + _dot(a_hi, b_lo)
    rhs = jnp.concatenate([jnp.concatenate([b_hi, b_lo], axis=1),
                           jnp.concatenate([b_hi, jnp.zeros_like(b_lo)], axis=1)], axis=0)
    out = _dot(lhs, rhs)
    return out[:, :n] + out[:, n:]


def _mm3_nt(a, b):
    a_hi, a_lo = _split_bf16(a, 2)
    b_hi, b_lo = _split_bf16(b, 2)
    lhs = jnp.concatenate([a_hi, a_lo], axis=1)
    rhs = jnp.concatenate([jnp.concatenate([b_hi, b_hi], axis=1),
                           jnp.concatenate([b_lo, jnp.zeros_like(b_lo)], axis=1)], axis=0)
    out = _dot_nt(lhs, rhs)
    n = b.shape[0]
    return out[:, :n] + out[:, n:]


def _sigmoid(x):
    return 1.0 / (1.0 + jnp.exp(-x))


def _silu(x):
    return x * _sigmoid(x)


def _layer_norm(y, g, b):
    mean = jnp.mean(y, -1, keepdims=True)
    d = y - mean
    var = jnp.mean(d * d, -1, keepdims=True)
    return d * lax.rsqrt(var + LN_EPS) * g + b


def _ada_kernel(c_ref, w_ref, b_ref, o_ref):
    o_ref[0] = _dot_hi(_silu(c_ref[...]), w_ref[0]) + b_ref[0]


def _ada(c, w_ada, b_ada):
    depth, d, n = w_ada.shape
    b = SUBLANES * pl.cdiv(c.shape[0], SUBLANES)
    c = jnp.pad(c, ((0, b - c.shape[0]), (0, 0)))
    tn = 1024
    return pl.pallas_call(
        _ada_kernel,
        grid=(depth, n // tn),
        in_specs=[pl.BlockSpec((b, d), lambda l, j: (0, 0)),
                  pl.BlockSpec((1, d, tn), lambda l, j: (l, 0, j)),
                  pl.BlockSpec((1, 1, tn), lambda l, j: (l, 0, j))],
        out_specs=pl.BlockSpec((1, b, tn), lambda l, j: (l, 0, j)),
        out_shape=jax.ShapeDtypeStruct((depth, b, n), F32),
        compiler_params=_cparams("parallel", "parallel"),
        name="ada",
    )(c, w_ada, b_ada.reshape(depth, 1, n))


def _ffn_kernel(x_ref, mod_ref, wg_ref, wu_ref, wo_ref, lng_ref, lnb_ref, o_ref,
                h_ref, acc_ref, *, mod_base, alpha):
    j = pl.program_id(1)

    @pl.when(j == 0)
    def _():
        sh = mod_ref[0, mod_base:mod_base + 1, :]
        sc = mod_ref[0, mod_base + 1:mod_base + 2, :]
        h_ref[...] = (x_ref[...] * (1.0 + sc) + sh).astype(BF16)
        acc_ref[...] = jnp.zeros_like(acc_ref)

    h = h_ref[...]
    gate = _dot(h, wg_ref[0])
    up = _dot(h, wu_ref[0])
    act = (_silu(gate) * up).astype(BF16)
    acc_ref[...] += _dot(act, wo_ref[0])

    @pl.when(j == pl.num_programs(1) - 1)
    def _():
        g = mod_ref[0, mod_base + 2:mod_base + 3, :]
        y = alpha * x_ref[...] + (FFN_RES * (1.0 + g)) * acc_ref[...]
        o_ref[...] = _layer_norm(y, lng_ref[...], lnb_ref[...])


def _ffn(x, mod, w_in, w_out, ln_g, ln_b, *, layer, mod_base, alpha, seq):
    t, d = x.shape
    f = w_out.shape[1]
    tm = min(TOKEN_TILE, seq)
    tf = FFN_TILE
    nf = f // tf
    per_b = seq // tm
    kern = functools.partial(_ffn_kernel, mod_base=mod_base, alpha=alpha)
    return pl.pallas_call(
        kern,
        grid=(t // tm, nf),
        in_specs=[pl.BlockSpec((tm, d), lambda i, j: (i, 0)),
                  pl.BlockSpec((1, N_MOD, d), lambda i, j: (i // per_b, 0, 0)),
                  pl.BlockSpec((1, d, tf), lambda i, j: (layer, 0, j)),
                  pl.BlockSpec((1, d, tf), lambda i, j: (layer, 0, nf + j)),
                  pl.BlockSpec((1, tf, d), lambda i, j: (layer, j, 0)),
                  pl.BlockSpec((1, d), lambda i, j: (0, 0)),
                  pl.BlockSpec((1, d), lambda i, j: (0, 0))],
        out_specs=pl.BlockSpec((tm, d), lambda i, j: (i, 0)),
        out_shape=jax.ShapeDtypeStruct((t, d), F32),
        scratch_shapes=[pltpu.VMEM((tm, d), BF16), pltpu.VMEM((tm, d), F32)],
        compiler_params=_cparams("parallel", "arbitrary"),
        name="ffn",
    )(x, mod, w_in, w_in, w_out, ln_g, ln_b)


def _mod_matmul_kernel(x_ref, mod_ref, w_ref, o_ref, *, mod_base):
    sh = mod_ref[0, mod_base:mod_base + 1, :]
    sc = mod_ref[0, mod_base + 1:mod_base + 2, :]
    h = (x_ref[...] * (1.0 + sc) + sh).astype(BF16)
    o_ref[...] = _dot(h, w_ref[0])


def _mod_matmul(x, mod, w, *, layer, n, tn, mod_base, seq):
    t, d = x.shape
    tm = min(TOKEN_TILE, seq)
    per_b = seq // tm
    kern = functools.partial(_mod_matmul_kernel, mod_base=mod_base)
    return pl.pallas_call(
        kern,
        grid=(n // tn, t // tm),
        in_specs=[pl.BlockSpec((tm, d), lambda j, i: (i, 0)),
                  pl.BlockSpec((1, N_MOD, d), lambda j, i: (i // per_b, 0, 0)),
                  pl.BlockSpec((1, d, tn), lambda j, i: (layer, 0, j))],
        out_specs=pl.BlockSpec((tm, tn), lambda j, i: (i, j)),
        out_shape=jax.ShapeDtypeStruct((t, n), F32),
        compiler_params=_cparams("parallel", "parallel"),
        name="in_proj",
    )(x, mod, w)


def _rwkv_kernel(r_ref, k_ref, v_ref, lo_ref, mur_ref, muk_ref, muv_ref, mulo_ref,
                 w0_ref, w2_ref, a0_ref, a2_ref, g2_ref, kk_ref, ka_ref, rk_ref,
                 lng_ref, lnb_ref, o_ref,
                 state_ref, pr_ref, pk_ref, pv_ref, plo_ref, obuf_ref, *, bb, tb):
    j = pl.program_id(1)
    L = RW_CHUNK
    nc = tb // L

    @pl.when(j == 0)
    def _():
        state_ref[...] = jnp.zeros_like(state_ref)
        pr_ref[...] = jnp.zeros_like(pr_ref)
        pk_ref[...] = jnp.zeros_like(pk_ref)
        pv_ref[...] = jnp.zeros_like(pv_ref)
        plo_ref[...] = jnp.zeros_like(plo_ref)

    lane_r = lax.broadcasted_iota(jnp.int32, (LANES, LANES), 0)
    lane_c = lax.broadcasted_iota(jnp.int32, (LANES, LANES), 1)

    def same_block(size):
        return (lane_r // size) == (lane_c // size)

    same_head = same_block(RW_HEAD_DIM).astype(BF16)
    stack_mask = ((lane_r // L) == (lane_c // RW_HEAD_DIM)).astype(F32)
    strict_lower = lane_r > lane_c
    lower = lane_r >= lane_c
    eye = (lane_r == lane_c).astype(F32)
    tri = (lax.broadcasted_iota(jnp.int32, (L, L), 0)
           >= lax.broadcasted_iota(jnp.int32, (L, L), 1)).astype(BF16)
    zeros = jnp.zeros((LANES, LANES), F32)

    def head_sum(x):
        return _dot_split_lhs(x, same_head, 2)

    def stack(x):
        return jnp.concatenate([x, x], axis=0) * stack_mask

    def prologue(b):
        def shift_mix(p_ref, prev_ref, mu_ref):
            p = p_ref[b]
            row = lax.broadcasted_iota(jnp.int32, p.shape, 0)
            prev = jnp.where(row == 0, prev_ref[b, 0:1, :], pltpu.roll(p, 1, axis=0))
            prev_ref[b, 0:1, :] = p[tb - 1:tb, :]
            return p + (prev - p) * mu_ref[...]

        r = shift_mix(r_ref, pr_ref, mur_ref)
        k = shift_mix(k_ref, pk_ref, muk_ref)
        v = shift_mix(v_ref, pv_ref, muv_ref)
        lo = shift_mix(lo_ref, plo_ref, mulo_ref)
        w_lo = lo[:, :RW_DECAY_LORA]
        a_lo = lo[:, RW_DECAY_LORA:RW_DECAY_LORA + RW_AAA_LORA]
        g_lo = lo[:, RW_DECAY_LORA + RW_AAA_LORA:]
        z = w0_ref[...] + _rw_dot(jnp.tanh(w_lo), w2_ref[...])
        softplus_neg_z = jnp.maximum(-z, 0.0) + jnp.log(1.0 + jnp.exp(-jnp.abs(z)))
        log_decay = -jnp.exp(-softplus_neg_z - 0.5)
        a = _sigmoid(a0_ref[...] + _rw_dot(a_lo, a2_ref[...]))
        gate = _rw_dot(_sigmoid(g_lo), g2_ref[...])
        kk = k * kk_ref[...]
        kk = kk / jnp.maximum(jnp.sqrt(head_sum(kk * kk)), 1e-12)
        k = k * (1.0 + (a - 1.0) * ka_ref[...])
        return r, k, v, kk, a, gate, log_decay

    seqs = [prologue(b) for b in range(bb)]
    items = [(b, c) for c in range(nc) for b in range(bb)]
    every = range(len(items))

    def chunk_operands(b, c):
        r, k, v, kk, a, _, log_decay = seqs[b]
        rows = slice(c * L, (c + 1) * L)
        lw = log_decay[rows]
        cum = _dot_split_rhs(tri, lw, 3)
        w_cum = jnp.exp(cum)
        w_inv = jnp.exp(-cum)
        w_prev = jnp.exp(cum - lw)
        w_last = w_cum[L - 1:L, :]
        kk_c = kk[rows]
        return (stack(-kk_c * w_prev), stack(kk_c * a[rows] * w_inv), stack(k[rows] * w_inv),
                stack(r[rows] * w_cum), stack(v[rows]), w_last)

    ops = [chunk_operands(b, c) for b, c in items]
    a2_ = [o[0] for o in ops]
    b2_ = [o[1] for o in ops]
    k2_ = [o[2] for o in ops]
    r2_ = [o[3] for o in ops]
    v2_ = [o[4] for o in ops]
    w_last = [o[5] for o in ops]

    scores = [_mm3_nt(jnp.concatenate([a2_[i], r2_[i]], axis=0),
                      jnp.concatenate([b2_[i], k2_[i]], axis=0)) for i in every]
    a_ab = [jnp.where(strict_lower, s[:LANES, :LANES], 0.0) for s in scores]
    a_ak = [jnp.where(strict_lower, s[:LANES, LANES:], 0.0) for s in scores]
    a_rb = [jnp.where(lower, s[LANES:, :LANES], 0.0) for s in scores]
    a_rk = [jnp.where(lower, s[LANES:, LANES:], 0.0) for s in scores]

    diag_blocks = same_block(RW_INV_BASE)
    p = [jnp.where(diag_blocks, x, 0.0) for x in a_ab]
    t = [eye + x for x in p]
    n = 2
    while n < RW_INV_BASE:
        p = [_mm3(x, x) for x in p]
        t = [t[i] + _mm3(t[i], p[i]) for i in every]
        n *= 2
    size = 2 * RW_INV_BASE
    while size <= L:
        off_diag = same_block(size) & ~same_block(size // 2)
        et = [_mm3(jnp.where(off_diag, a_ab[i], 0.0), t[i]) for i in every]
        t = [t[i] + _mm3(t[i], et[i]) for i in every]
        size *= 2

    akv = [_mm3(a_ak[i], v2_[i]) for i in every]
    x = [_mm3(t[i], jnp.concatenate([a2_[i], akv[i]], axis=1)) for i in every]
    big = []
    for i in every:
        lhs = jnp.concatenate(
            [jnp.concatenate([a_rb[i], a_rk[i]], axis=1),
             jnp.concatenate([(b2_[i] * w_last[i]).T, (k2_[i] * w_last[i]).T], axis=1)], axis=0)
        rhs = jnp.concatenate([x[i], jnp.concatenate([zeros, v2_[i]], axis=1)], axis=0)
        big.append(_mm3(lhs, rhs))

    states = [state_ref[b] for b in range(bb)]
    for i, (b, c) in enumerate(items):
        r_hat = r2_[i] + big[i][:LANES, :LANES]
        st = _mm3(jnp.concatenate([r_hat, big[i][LANES:, :LANES]], axis=0), states[b])
        o2 = st[:LANES] + big[i][:LANES, LANES:]
        obuf_ref[b, c * L:(c + 1) * L, :] = o2[:L] + o2[L:]
        w_rows = jnp.broadcast_to(w_last[i], (LANES, LANES)).T
        states[b] = states[b] * w_rows + st[LANES:] + big[i][LANES:, LANES:]
    for b in range(bb):
        state_ref[b] = states[b]

    inv_n = 1.0 / RW_HEAD_DIM
    for b in range(bb):
        r, k, v, _, _, gate, _ = seqs[b]
        o = obuf_ref[b]
        mean = head_sum(o) * inv_n
        d = o - mean
        var = head_sum(d * d) * inv_n
        on = d * lax.rsqrt(var + RW_GN_EPS) * lng_ref[...] + lnb_ref[...]
        bonus = head_sum(r * k * rk_ref[...]) * v
        o_ref[b] = ((on + bonus) * gate).astype(o_ref.dtype)


def _rwkv(p_rw, mu, w0, w2, a0, a2, g2, k_k, k_a, r_k, lnx_g, lnx_b, *, batch, seq):
    width = w0.shape[-1]
    pairs = width // LANES
    tb = min(RW_BLOCK, seq)
    lo_w = RW_DECAY_LORA + RW_AAA_LORA + RW_GATE_LORA
    lo_blk = 3 * width // lo_w
    p3 = p_rw.reshape(batch, seq, p_rw.shape[-1])

    def col(off):
        return lambda h, j: (0, j, off + h)

    def vec(off):
        return lambda h, j: (0, off + h)

    row = lambda a: a.reshape(1, -1)
    kern = functools.partial(_rwkv_kernel, bb=batch, tb=tb)
    out = pl.pallas_call(
        kern,
        grid=(pairs, seq // tb),
        in_specs=[pl.BlockSpec((batch, tb, LANES), col(0)),
                  pl.BlockSpec((batch, tb, LANES), col(pairs)),
                  pl.BlockSpec((batch, tb, LANES), col(2 * pairs)),
                  pl.BlockSpec((batch, tb, lo_w), lambda h, j: (0, j, lo_blk)),
                  pl.BlockSpec((1, LANES), vec(0)),
                  pl.BlockSpec((1, LANES), vec(pairs)),
                  pl.BlockSpec((1, LANES), vec(2 * pairs)),
                  pl.BlockSpec((1, lo_w), lambda h, j: (0, lo_blk)),
                  pl.BlockSpec((1, LANES), vec(0)),
                  pl.BlockSpec((RW_DECAY_LORA, LANES), vec(0)),
                  pl.BlockSpec((1, LANES), vec(0)),
                  pl.BlockSpec((RW_AAA_LORA, LANES), vec(0)),
                  pl.BlockSpec((RW_GATE_LORA, LANES), vec(0)),
                  pl.BlockSpec((1, LANES), vec(0)),
                  pl.BlockSpec((1, LANES), vec(0)),
                  pl.BlockSpec((1, LANES), vec(0)),
                  pl.BlockSpec((1, LANES), vec(0)),
                  pl.BlockSpec((1, LANES), vec(0))],
        out_specs=pl.BlockSpec((batch, tb, LANES), lambda h, j: (0, j, h)),
        out_shape=jax.ShapeDtypeStruct((batch, seq, width), RW_OUT_DTYPE),
        scratch_shapes=[pltpu.VMEM((batch, LANES, LANES), F32),
                        pltpu.VMEM((batch, SUBLANES, LANES), F32),
                        pltpu.VMEM((batch, SUBLANES, LANES), F32),
                        pltpu.VMEM((batch, SUBLANES, LANES), F32),
                        pltpu.VMEM((batch, SUBLANES, lo_w), F32),
                        pltpu.VMEM((batch, tb, LANES), F32)],
        compiler_params=_cparams("parallel", "arbitrary"),
        name="rwkv",
    )(p3, p3, p3, p3, row(mu), row(mu), row(mu), row(mu), row(w0), w2, row(a0), a2, g2,
      row(k_k), row(k_a), row(r_k), row(lnx_g), row(lnx_b))
    return out.reshape(batch * seq, width)


def _rope_table_kernel(pos_ref, freq_ref, cos_ref, sa_ref, sb_ref):
    half = MLA_ROPE // 2
    ang = pos_ref[...].astype(F32) * freq_ref[...]
    lane = lax.broadcasted_iota(jnp.int32, ang.shape, 1)
    cos = jnp.cos(ang)
    sin = jnp.sin(ang)
    cos_ref[...] = jnp.where(lane < MLA_ROPE, cos, 1.0)
    sa_ref[...] = jnp.where(lane < half, -sin, 0.0)
    sb_ref[...] = jnp.where((lane >= half) & (lane < MLA_ROPE), sin, 0.0)


def _rope_tables(positions):
    t = positions.size
    half = MLA_ROPE // 2
    tm = min(2048, t)
    inv_freq = ROPE_THETA ** (-jnp.arange(half, dtype=F32) / half)
    freq = jnp.concatenate([inv_freq, inv_freq, jnp.zeros((LANES - MLA_ROPE,), F32)])
    spec = pl.BlockSpec((tm, LANES), lambda i: (i, 0))
    shp = jax.ShapeDtypeStruct((t, LANES), F32)
    return pl.pallas_call(
        _rope_table_kernel,
        grid=(t // tm,),
        in_specs=[pl.BlockSpec((tm, 1), lambda i: (i, 0)),
                  pl.BlockSpec((1, LANES), lambda i: (0, 0))],
        out_specs=[spec, spec, spec],
        out_shape=[shp, shp, shp],
        compiler_params=_cparams("parallel"),
        name="rope_tables",
    )(positions.reshape(t, 1), freq.reshape(1, LANES))


def _rope(x, cos, sa, sb):
    half = MLA_ROPE // 2
    return x * cos + pltpu.roll(x, LANES - half, axis=1) * sa + pltpu.roll(x, half, axis=1) * sb


def _mla_up_kernel(p_ref, cos_ref, sa_ref, sb_ref, qg_ref, wq_ref, kvg_ref, wk_ref, wvt_ref,
                   q_ref, k_ref, vt_ref, *, heads):
    p = p_ref[...]
    cos, sa, sb = cos_ref[...], sa_ref[...], sb_ref[...]

    def rms(x, g):
        return x * lax.rsqrt(jnp.mean(x * x, -1, keepdims=True) + RMS_EPS) * g

    q_lat = rms(p[:, :MLA_Q_LORA], qg_ref[...]).astype(BF16)
    kv_lat = rms(p[:, MLA_Q_LORA:MLA_Q_LORA + MLA_KV_LORA], kvg_ref[...]).astype(BF16)
    k_pe = p[:, MLA_Q_LORA + MLA_KV_LORA:]
    k_pe = jnp.concatenate([k_pe, jnp.zeros((k_pe.shape[0], LANES - MLA_ROPE), F32)], axis=1)
    k_pe = _rope(k_pe, cos, sa, sb).astype(BF16)

    q = _dot(q_lat, wq_ref[...]) * MLA_SCALE
    k_nope = _dot(kv_lat, wk_ref[...])
    vt_ref[...] = _dot_nt(wvt_ref[...], kv_lat).astype(BF16)
    for h in range(heads):
        base = h * MLA_QK_PAD
        q_ref[:, base:base + MLA_NOPE] = q[:, base:base + MLA_NOPE].astype(BF16)
        q_ref[:, base + MLA_NOPE:base + MLA_QK_PAD] = _rope(
            q[:, base + MLA_NOPE:base + MLA_QK_PAD], cos, sa, sb).astype(BF16)
        k_ref[:, base:base + MLA_NOPE] = k_nope[:, h * MLA_NOPE:(h + 1) * MLA_NOPE].astype(BF16)
        k_ref[:, base + MLA_NOPE:base + MLA_QK_PAD] = k_pe


def _mla_up(p_mla, tables, q_norm_g, w_uq, kv_norm_g, w_ukv, *, seq):
    t, cols = p_mla.shape
    heads = w_uq.shape[1] // (MLA_NOPE + MLA_ROPE)
    tm = min(TOKEN_TILE, seq)
    wq = w_uq.reshape(MLA_Q_LORA, heads, MLA_NOPE + MLA_ROPE)
    wq = jnp.pad(wq, ((0, 0), (0, 0), (0, MLA_QK_PAD - MLA_NOPE - MLA_ROPE)))
    wq = wq.reshape(MLA_Q_LORA, heads * MLA_QK_PAD).astype(BF16)
    wkv = w_ukv.reshape(MLA_KV_LORA, heads, MLA_NOPE + MLA_V)
    wk = wkv[:, :, :MLA_NOPE].reshape(MLA_KV_LORA, heads * MLA_NOPE).astype(BF16)
    wvt = wkv[:, :, MLA_NOPE:].reshape(MLA_KV_LORA, heads * MLA_V).T.astype(BF16)
    tok = lambda n: pl.BlockSpec((tm, n), lambda i: (i, 0))
    full = lambda a: pl.BlockSpec(a.shape, lambda i: (0, 0))
    qg = q_norm_g.reshape(1, -1)
    kvg = kv_norm_g.reshape(1, -1)
    kern = functools.partial(_mla_up_kernel, heads=heads)
    return pl.pallas_call(
        kern,
        grid=(t // tm,),
        in_specs=[tok(cols), tok(LANES), tok(LANES), tok(LANES),
                  full(qg), full(wq), full(kvg), full(wk), full(wvt)],
        out_specs=[tok(heads * MLA_QK_PAD), tok(heads * MLA_QK_PAD),
                   pl.BlockSpec((heads * MLA_V, tm), lambda i: (0, i))],
        out_shape=[jax.ShapeDtypeStruct((t, heads * MLA_QK_PAD), BF16),
                   jax.ShapeDtypeStruct((t, heads * MLA_QK_PAD), BF16),
                   jax.ShapeDtypeStruct((heads * MLA_V, t), BF16)],
        compiler_params=_cparams("parallel"),
        name="mla_up",
    )(p_mla, *tables, qg, wq, kvg, wk, wvt)


def _attn_kernel(qi_ref, kj_ref, q_ref, k_ref, vt_ref, o_ref, m_ref, l_ref, acc_ref, *, heads):
    step_id = pl.program_id(1)
    i = qi_ref[step_id]
    j = kj_ref[step_id]

    @pl.when(j == 0)
    def _():
        m_ref[...] = jnp.full_like(m_ref, -jnp.inf)
        l_ref[...] = jnp.zeros_like(l_ref)
        acc_ref[...] = jnp.zeros_like(acc_ref)

    def qk(h):
        return slice(h * MLA_QK_PAD, (h + 1) * MLA_QK_PAD)

    def vo(h):
        return slice(h * MLA_V, (h + 1) * MLA_V)

    def step(masked):
        hs = range(heads)
        s = [_dot_nt(k_ref[:, qk(h)], q_ref[:, qk(h)]) for h in hs]
        if masked:
            key = lax.broadcasted_iota(jnp.int32, s[0].shape, 0)
            query = lax.broadcasted_iota(jnp.int32, s[0].shape, 1)
            s = [jnp.where(key <= query, x, -jnp.inf) for x in s]
        m_prev = [m_ref[h] for h in hs]
        m_new = [jnp.maximum(m_prev[h], jnp.max(s[h], 0, keepdims=True)) for h in hs]
        alpha = [jnp.exp(m_prev[h] - m_new[h]) for h in hs]
        p = [jnp.exp(s[h] - m_new[h]) for h in hs]
        pv = [_dot(vt_ref[vo(h), :], p[h].astype(BF16)) for h in hs]
        for h in hs:
            l_ref[h] = alpha[h] * l_ref[h] + jnp.sum(p[h], 0, keepdims=True)
            acc_ref[vo(h), :] = alpha[h] * acc_ref[vo(h), :] + pv[h]
            m_ref[h] = m_new[h]

    @pl.when(j < i)
    def _():
        step(False)

    @pl.when(j == i)
    def _():
        step(True)
        for h in range(heads):
            o_ref[:, vo(h)] = (acc_ref[vo(h), :] / l_ref[h]).T.astype(o_ref.dtype)


def _attention(q, k, vt, *, batch, seq):
    t = q.shape[0]
    heads = vt.shape[0] // MLA_V
    tq = min(ATTN_TILE, seq)
    nq = seq // tq
    pairs = [(i, j) for i in range(nq) for j in range(i + 1)]
    qi = jnp.asarray([p[0] for p in pairs], jnp.int32)
    kj = jnp.asarray([p[1] for p in pairs], jnp.int32)
    kern = functools.partial(_attn_kernel, heads=heads)
    grid_spec = pltpu.PrefetchScalarGridSpec(
        num_scalar_prefetch=2,
        grid=(batch, len(pairs)),
        in_specs=[pl.BlockSpec((tq, heads * MLA_QK_PAD), lambda b, s, qi, kj: (b * nq + qi[s], 0)),
                  pl.BlockSpec((tq, heads * MLA_QK_PAD), lambda b, s, qi, kj: (b * nq + kj[s], 0)),
                  pl.BlockSpec((heads * MLA_V, tq), lambda b, s, qi, kj: (0, b * nq + kj[s]))],
        out_specs=pl.BlockSpec((tq, heads * MLA_V), lambda b, s, qi, kj: (b * nq + qi[s], 0)),
        scratch_shapes=[pltpu.VMEM((heads, 1, tq), F32), pltpu.VMEM((heads, 1, tq), F32),
                        pltpu.VMEM((heads * MLA_V, tq), F32)])
    return pl.pallas_call(
        kern,
        grid_spec=grid_spec,
        out_shape=jax.ShapeDtypeStruct((t, heads * MLA_V), BF16),
        compiler_params=_cparams("parallel", "arbitrary"),
        name="attention",
    )(qi, kj, q, k, vt)


def _conv_kernel(lin_ref, gate_ref, w_ref, b_ref, lng_ref, lnb_ref, o_ref, u_ref, *, tm):
    j = pl.program_id(1)

    @pl.when(j == 0)
    def _():
        u_ref[0:CONV_HALO, :] = jnp.zeros((CONV_HALO, u_ref.shape[1]), F32)

    u_ref[CONV_HALO:, :] = lin_ref[0] * _sigmoid(gate_ref[0])
    first = CONV_HALO - (CONV_K - 1)
    acc = jnp.zeros((tm, u_ref.shape[1]), F32)
    for tap in range(CONV_K):
        acc = acc + u_ref[first + tap:first + tap + tm, :] * w_ref[tap:tap + 1, :]
    halo = u_ref[tm:tm + CONV_HALO, :]
    u_ref[0:CONV_HALO, :] = halo
    y = _layer_norm(acc + b_ref[...], lng_ref[...], lnb_ref[...])
    o_ref[0] = _silu(y).astype(o_ref.dtype)


def _conv(p_conv, conv_w, conv_b, ln_g, ln_b, *, batch, seq):
    ch = conv_w.shape[1]
    tm = min(TOKEN_TILE, seq)
    p3 = p_conv.reshape(batch, seq, 2 * ch)
    full = lambda a: pl.BlockSpec(a.shape, lambda b, j: (0, 0))
    row = lambda a: a.reshape(1, -1)
    kern = functools.partial(_conv_kernel, tm=tm)
    out = pl.pallas_call(
        kern,
        grid=(batch, seq // tm),
        in_specs=[pl.BlockSpec((1, tm, ch), lambda b, j: (b, j, 0)),
                  pl.BlockSpec((1, tm, ch), lambda b, j: (b, j, 1)),
                  full(conv_w), full(row(conv_b)), full(row(ln_g)), full(row(ln_b))],
        out_specs=pl.BlockSpec((1, tm, ch), lambda b, j: (b, j, 0)),
        out_shape=jax.ShapeDtypeStruct((batch, seq, ch), BF16),
        scratch_shapes=[pltpu.VMEM((CONV_HALO + tm, ch), F32)],
        compiler_params=_cparams("parallel", "arbitrary"),
        name="conv",
    )(p3, p3, conv_w, row(conv_b), row(ln_g), row(ln_b))
    return out.reshape(batch * seq, ch)


def _out_proj_kernel(x_ref, mod_ref, y1_ref, y2_ref, y3_ref, w1_ref, w2_ref, w3_ref,
                     lng_ref, lnb_ref, o_ref, *, mod_base, alpha):
    y = (_dot(y1_ref[...], w1_ref[0]) + _dot(y2_ref[...], w2_ref[0])
         + _dot(y3_ref[...], w3_ref[0]))
    g = mod_ref[0, mod_base + 2:mod_base + 3, :]
    o_ref[...] = _layer_norm(alpha * x_ref[...] + (1.0 + g) * y, lng_ref[...], lnb_ref[...])


def _out_proj(x, mod, ys, w_out, ln_g, ln_b, *, layer, mod_base, alpha, seq):
    t, d = x.shape
    tm = min(TOKEN_TILE, seq)
    per_b = seq // tm
    tok = lambda n: pl.BlockSpec((tm, n), lambda i: (i, 0))
    full = lambda a: pl.BlockSpec(a.shape, lambda i: (0, 0))
    w_specs = []
    offset = 0
    for y in ys:
        width = y.shape[1]
        assert offset % width == 0
        w_specs.append(pl.BlockSpec((1, width, d), functools.partial(
            lambda i, blk: (layer, blk, 0), blk=offset // width)))
        offset += width
    assert offset == w_out.shape[1]
    kern = functools.partial(_out_proj_kernel, mod_base=mod_base, alpha=alpha)
    return pl.pallas_call(
        kern,
        grid=(t // tm,),
        in_specs=[tok(d), pl.BlockSpec((1, N_MOD, d), lambda i: (i // per_b, 0, 0)),
                  *[tok(y.shape[1]) for y in ys], *w_specs, full(ln_g), full(ln_b)],
        out_specs=tok(d),
        out_shape=jax.ShapeDtypeStruct((t, d), F32),
        compiler_params=_cparams("parallel"),
        name="out_proj",
    )(x, mod, *ys, w_out, w_out, w_out, ln_g, ln_b)


def kernel(x, c, positions, w_ada, b_ada, ln_g, ln_b, w_ffn1_in, w_ffn1_out, w_ffn2_in, w_ffn2_out, w_in, w_out, rw_mu, rw_w0, rw_w2, rw_a0, rw_a2, rw_g2, rw_k_k, rw_k_a, rw_r_k, rw_lnx_g, rw_lnx_b, mla_q_norm_g, mla_w_uq, mla_kv_norm_g, mla_w_ukv, conv_w, conv_b, conv_ln_g, conv_ln_b):
    batch, seq, d = x.shape
    depth = w_ada.shape[0]
    alpha = (2 * depth) ** 0.25
    rw_cols = rw_mu.shape[1]
    mla_cols = MLA_Q_LORA + MLA_KV_LORA + MLA_ROPE
    conv_cols = w_in.shape[2] - rw_cols - mla_cols

    mod_all = _ada(c, w_ada, b_ada)[:, :batch].reshape(depth, batch, N_MOD, d)
    tables = _rope_tables(positions)
    xt = x.reshape(batch * seq, d)
    row = lambda a: a.reshape(1, -1)
    w_ffn1_in, w_ffn1_out, w_ffn2_in, w_ffn2_out, w_in, w_out = (
        w.astype(BF16) for w in (w_ffn1_in, w_ffn1_out, w_ffn2_in, w_ffn2_out, w_in, w_out))

    for l in range(depth):
        mod = mod_all[l]
        xt = _ffn(xt, mod, w_ffn1_in, w_ffn1_out, row(ln_g[l, 0]), row(ln_b[l, 0]),
                  layer=l, mod_base=0, alpha=alpha, seq=seq)

        p_rw = _mod_matmul(xt, mod, w_in, layer=l, n=rw_cols, tn=rw_cols // 2,
                           mod_base=3, seq=seq)
        p_mla = _mod_matmul(xt, mod, w_in[l:l + 1, :, rw_cols:rw_cols + mla_cols], layer=0,
                            n=mla_cols, tn=mla_cols, mod_base=3, seq=seq)
        p_conv = _mod_matmul(xt, mod, w_in[l:l + 1, :, rw_cols + mla_cols:], layer=0,
                             n=conv_cols, tn=conv_cols, mod_base=3, seq=seq)

        y_rw = _rwkv(p_rw, rw_mu[l], rw_w0[l], rw_w2[l], rw_a0[l], rw_a2[l], rw_g2[l],
                     rw_k_k[l], rw_k_a[l], rw_r_k[l], rw_lnx_g[l], rw_lnx_b[l],
                     batch=batch, seq=seq)
        q, k, v = _mla_up(p_mla, tables, mla_q_norm_g[l], mla_w_uq[l], mla_kv_norm_g[l],
                          mla_w_ukv[l], seq=seq)
        y_mla = _attention(q, k, v, batch=batch, seq=seq)
        y_conv = _conv(p_conv, conv_w[l], conv_b[l], conv_ln_g[l], conv_ln_b[l],
                       batch=batch, seq=seq)

        xt = _out_proj(xt, mod, (y_rw, y_mla, y_conv), w_out, row(ln_g[l, 1]), row(ln_b[l, 1]),
                       layer=l, mod_base=3, alpha=alpha, seq=seq)

        xt = _ffn(xt, mod, w_ffn2_in, w_ffn2_out, row(ln_g[l, 2]), row(ln_b[l, 2]),
                  layer=l, mod_base=6, alpha=alpha, seq=seq)
    return xt.reshape(batch, seq, d)
```

```python
import functools

import jax
import jax.numpy as jnp
from jax import lax
from jax.experimental import pallas as pl
from jax.experimental.pallas import tpu as pltpu

F32 = jnp.float32
BF16 = jnp.bfloat16

LANES = 128
SUBLANES = 8
RW_HEAD_DIM = 64
RW_CHUNK = 64
RW_INV_BASE = 16
RW_OUT_DTYPE = jnp.bfloat16
RW_DECAY_LORA = 64
RW_AAA_LORA = 64
RW_GATE_LORA = 128
RW_GN_EPS = 64e-5
MLA_NOPE = 128
MLA_ROPE = 64
MLA_V = 128
MLA_Q_LORA = 384
MLA_KV_LORA = 256
MLA_QK_PAD = 256
MLA_SCALE = (MLA_NOPE + MLA_ROPE) ** -0.5
ROPE_THETA = 10000.0
CONV_K = 31
CONV_HALO = 32
N_MOD = 9
FFN_RES = 0.5
LN_EPS = 1e-5
RMS_EPS = 1e-6
VMEM_LIMIT = 52 * 1024 * 1024
TOKEN_TILE = 512
ATTN_TILE = 512
RW_BLOCK = 512
FFN_TILE = 256
FFN_TOKEN_TILE = 1024


def _cparams(*sem):
    return pltpu.CompilerParams(dimension_semantics=sem, vmem_limit_bytes=VMEM_LIMIT)


def _dot(a, b):
    return jnp.dot(a, b, preferred_element_type=F32)


def _dot_nt(a, b):
    return lax.dot_general(a, b, (((1,), (1,)), ((), ())), preferred_element_type=F32)


def _rw_dot(a, b):
    return _dot(a.astype(BF16), b.astype(BF16))


def _split_bf16(x, terms):
    parts = []
    for _ in range(terms):
        hi = x.astype(BF16)
        parts.append(hi)
        x = x - hi.astype(F32)
    return parts


def _dot_split_lhs(x, exact_rhs, terms):
    return sum(_dot(part, exact_rhs) for part in _split_bf16(x, terms))


def _dot_split_rhs(exact_lhs, x, terms):
    return sum(_dot(exact_lhs, part) for part in _split_bf16(x, terms))


def _mm3(a, b):
    a_hi, a_lo = _split_bf16(a, 2)
    b_hi, b_lo = _split_bf16(b, 2)
    k, n = b.shape
    if k != LANES:
        return _dot(a_hi, b_hi) + _dot(a_lo, b_hi) + _dot(a_hi, b_lo)
    lhs = jnp.concatenate([a_hi, a_lo], axis=1)
    if n != LANES:
        return _dot(lhs, jnp.concatenate([b_hi, b_hi], axis=0)) + _dot(a_hi, b_lo)
    rhs = jnp.concatenate([jnp.concatenate([b_hi, b_lo], axis=1),
                           jnp.concatenate([b_hi, jnp.zeros_like(b_lo)], axis=1)], axis=0)
    out = _dot(lhs, rhs)
    return out[:, :n] + out[:, n:]


def _sigmoid(x):
    return 1.0 / (1.0 + jnp.exp(-x))


def _silu(x):
    return x * _sigmoid(x)


def _layer_norm(y, g, b):
    mean = jnp.mean(y, -1, keepdims=True)
    d = y - mean
    var = jnp.mean(d * d, -1, keepdims=True)
    return d * lax.rsqrt(var + LN_EPS) * g + b


def _ada_kernel(ct_ref, w_ref, b_ref, o_ref, *, batch):
    w = w_ref[0]
    rows = [jnp.sum(_silu(ct_ref[:, b:b + 1]) * w, axis=0, keepdims=True) for b in range(batch)]
    o_ref[0] = jnp.concatenate(rows, axis=0) + b_ref[0]


def _ada(c, w_ada, b_ada):
    depth, d, n = w_ada.shape
    batch = c.shape[0]
    tn = 1024
    kern = functools.partial(_ada_kernel, batch=batch)
    return pl.pallas_call(
        kern,
        grid=(depth, n // tn),
        in_specs=[pl.BlockSpec((d, batch), lambda l, j: (0, 0)),
                  pl.BlockSpec((1, d, tn), lambda l, j: (l, 0, j)),
                  pl.BlockSpec((1, 1, tn), lambda l, j: (l, 0, j))],
        out_specs=pl.BlockSpec((1, batch, tn), lambda l, j: (l, 0, j)),
        out_shape=jax.ShapeDtypeStruct((depth, batch, n), F32),
        compiler_params=_cparams("parallel", "parallel"),
        name="ada",
    )(c.T, w_ada, b_ada.reshape(depth, 1, n))


def _ffn_kernel(x_ref, mod_ref, wg_ref, wu_ref, wo_ref, lng_ref, lnb_ref, o_ref,
                h_ref, *, mod_base, alpha):
    j = pl.program_id(1)

    @pl.when(j == 0)
    def _():
        sh = mod_ref[0, mod_base:mod_base + 1, :]
        sc = mod_ref[0, mod_base + 1:mod_base + 2, :]
        h_ref[...] = (x_ref[...] * (1.0 + sc) + sh).astype(BF16)
        o_ref[...] = jnp.zeros_like(o_ref)

    h = h_ref[...]
    gate = _dot(h, wg_ref[0])
    up = _dot(h, wu_ref[0])
    act = (_silu(gate) * up).astype(BF16)
    o_ref[...] += _dot(act, wo_ref[0])

    @pl.when(j == pl.num_programs(1) - 1)
    def _():
        g = mod_ref[0, mod_base + 2:mod_base + 3, :]
        y = alpha * x_ref[...] + (FFN_RES * (1.0 + g)) * o_ref[...]
        o_ref[...] = _layer_norm(y, lng_ref[...], lnb_ref[...])


def _ffn(x, mod, w_in, w_out, ln_g, ln_b, *, layer, mod_base, alpha, seq):
    t, d = x.shape
    f = w_out.shape[1]
    tm = min(FFN_TOKEN_TILE, seq)
    tf = FFN_TILE
    nf = f // tf
    per_b = seq // tm
    kern = functools.partial(_ffn_kernel, mod_base=mod_base, alpha=alpha)
    return pl.pallas_call(
        kern,
        grid=(t // tm, nf),
        in_specs=[pl.BlockSpec((tm, d), lambda i, j: (i, 0)),
                  pl.BlockSpec((1, N_MOD, d), lambda i, j: (i // per_b, 0, 0)),
                  pl.BlockSpec((1, d, tf), lambda i, j: (layer, 0, j)),
                  pl.BlockSpec((1, d, tf), lambda i, j: (layer, 0, nf + j)),
                  pl.BlockSpec((1, tf, d), lambda i, j: (layer, j, 0)),
                  pl.BlockSpec((1, d), lambda i, j: (0, 0)),
                  pl.BlockSpec((1, d), lambda i, j: (0, 0))],
        out_specs=pl.BlockSpec((tm, d), lambda i, j: (i, 0)),
        out_shape=jax.ShapeDtypeStruct((t, d), F32),
        scratch_shapes=[pltpu.VMEM((tm, d), BF16)],
        compiler_params=_cparams("parallel", "arbitrary"),
        name="ffn",
    )(x, mod, w_in, w_in, w_out, ln_g, ln_b)


def _mod_matmul_kernel(x_ref, mod_ref, w_ref, o_ref, *, mod_base):
    sh = mod_ref[0, mod_base:mod_base + 1, :]
    sc = mod_ref[0, mod_base + 1:mod_base + 2, :]
    h = (x_ref[...] * (1.0 + sc) + sh).astype(BF16)
    o_ref[...] = _dot(h, w_ref[0])


def _mod_matmul(x, mod, w, *, layer, n, tn, mod_base, seq):
    t, d = x.shape
    tm = min(TOKEN_TILE, seq)
    per_b = seq // tm
    kern = functools.partial(_mod_matmul_kernel, mod_base=mod_base)
    return pl.pallas_call(
        kern,
        grid=(n // tn, t // tm),
        in_specs=[pl.BlockSpec((tm, d), lambda j, i: (i, 0)),
                  pl.BlockSpec((1, N_MOD, d), lambda j, i: (i // per_b, 0, 0)),
                  pl.BlockSpec((1, d, tn), lambda j, i: (layer, 0, j))],
        out_specs=pl.BlockSpec((tm, tn), lambda j, i: (i, j)),
        out_shape=jax.ShapeDtypeStruct((t, n), F32),
        compiler_params=_cparams("parallel", "parallel"),
        name="in_proj",
    )(x, mod, w)


def _rwkv_kernel(r_ref, k_ref, v_ref, lo_ref, mur_ref, muk_ref, muv_ref, mulo_ref,
                 w0_ref, w2_ref, a0_ref, a2_ref, g2_ref, kk_ref, ka_ref, rk_ref,
                 lng_ref, lnb_ref, o_ref,
                 state_ref, pr_ref, pk_ref, pv_ref, plo_ref, obuf_ref, *, bb, tb):
    j = pl.program_id(1)
    L = RW_CHUNK
    nc = tb // L

    @pl.when(j == 0)
    def _():
        state_ref[...] = jnp.zeros_like(state_ref)
        pr_ref[...] = jnp.zeros_like(pr_ref)
        pk_ref[...] = jnp.zeros_like(pk_ref)
        pv_ref[...] = jnp.zeros_like(pv_ref)
        plo_ref[...] = jnp.zeros_like(plo_ref)

    lane_r = lax.broadcasted_iota(jnp.int32, (LANES, LANES), 0)
    lane_c = lax.broadcasted_iota(jnp.int32, (LANES, LANES), 1)

    def same_block(size):
        return (lane_r // size) == (lane_c // size)

    same_head = same_block(RW_HEAD_DIM).astype(BF16)
    stack_mask = ((lane_r // L) == (lane_c // RW_HEAD_DIM)).astype(F32)
    strict_lower = lane_r > lane_c
    lower = lane_r >= lane_c
    eye = (lane_r == lane_c).astype(F32)
    tri = (lax.broadcasted_iota(jnp.int32, (L, L), 0)
           >= lax.broadcasted_iota(jnp.int32, (L, L), 1)).astype(BF16)
    zeros = jnp.zeros((LANES, LANES), F32)

    def head_sum(x):
        return _dot_split_lhs(x, same_head, 2)

    def stack(x):
        return jnp.concatenate([x, x], axis=0) * stack_mask

    def prologue(b):
        def shift_mix(p_ref, prev_ref, mu_ref):
            p = p_ref[b]
            row = lax.broadcasted_iota(jnp.int32, p.shape, 0)
            prev = jnp.where(row == 0, prev_ref[b, 0:1, :], pltpu.roll(p, 1, axis=0))
            prev_ref[b, 0:1, :] = p[tb - 1:tb, :]
            return p + (prev - p) * mu_ref[...]

        r = shift_mix(r_ref, pr_ref, mur_ref)
        k = shift_mix(k_ref, pk_ref, muk_ref)
        v = shift_mix(v_ref, pv_ref, muv_ref)
        lo = shift_mix(lo_ref, plo_ref, mulo_ref)
        w_lo = lo[:, :RW_DECAY_LORA]
        a_lo = lo[:, RW_DECAY_LORA:RW_DECAY_LORA + RW_AAA_LORA]
        g_lo = lo[:, RW_DECAY_LORA + RW_AAA_LORA:]
        z = w0_ref[...] + _rw_dot(jnp.tanh(w_lo), w2_ref[...])
        softplus_neg_z = jnp.maximum(-z, 0.0) + jnp.log(1.0 + jnp.exp(-jnp.abs(z)))
        log_decay = -jnp.exp(-softplus_neg_z - 0.5)
        a = _sigmoid(a0_ref[...] + _rw_dot(a_lo, a2_ref[...]))
        gate = _rw_dot(_sigmoid(g_lo), g2_ref[...])
        kk = k * kk_ref[...]
        kk = kk / jnp.maximum(jnp.sqrt(head_sum(kk * kk)), 1e-12)
        k = k * (1.0 + (a - 1.0) * ka_ref[...])
        return r, k, v, kk, a, gate, log_decay

    seqs = [prologue(b) for b in range(bb)]
    items = [(b, c) for c in range(nc) for b in range(bb)]
    every = range(len(items))

    def chunk_operands(b, c):
        r, k, v, kk, a, _, log_decay = seqs[b]
        rows = slice(c * L, (c + 1) * L)
        lw = log_decay[rows]
        cum = _dot_split_rhs(tri, lw, 3)
        w_cum = jnp.exp(cum)
        w_inv = jnp.exp(-cum)
        w_prev = jnp.exp(cum - lw)
        w_last = w_cum[L - 1:L, :]
        kk_c = kk[rows]
        return (stack(-kk_c * w_prev), stack(kk_c * a[rows] * w_inv), stack(k[rows] * w_inv),
                stack(r[rows] * w_cum), stack(v[rows]), w_last)

    ops = [chunk_operands(b, c) for b, c in items]
    a2_ = [o[0] for o in ops]
    b2_ = [o[1] for o in ops]
    k2_ = [o[2] for o in ops]
    r2_ = [o[3] for o in ops]
    v2_ = [o[4] for o in ops]
    w_last = [o[5] for o in ops]

    scores = [_dot_nt(jnp.concatenate([a2_[i], r2_[i]], axis=0).astype(BF16),
                      jnp.concatenate([b2_[i], k2_[i]], axis=0).astype(BF16)) for i in every]
    a_ab = [jnp.where(strict_lower, s[:LANES, :LANES], 0.0) for s in scores]
    a_ak = [jnp.where(strict_lower, s[:LANES, LANES:], 0.0) for s in scores]
    a_rb = [jnp.where(lower, s[LANES:, :LANES], 0.0) for s in scores]
    a_rk = [jnp.where(lower, s[LANES:, LANES:], 0.0) for s in scores]

    diag_blocks = same_block(RW_INV_BASE)
    p = [jnp.where(diag_blocks, x, 0.0) for x in a_ab]
    t = [eye + x for x in p]
    n = 2
    while n < RW_INV_BASE:
        p = [_mm3(x, x) for x in p]
        t = [t[i] + _mm3(t[i], p[i]) for i in every]
        n *= 2
    size = 2 * RW_INV_BASE
    while size <= L:
        off_diag = same_block(size) & ~same_block(size // 2)
        et = [_mm3(jnp.where(off_diag, a_ab[i], 0.0), t[i]) for i in every]
        t = [t[i] + _mm3(t[i], et[i]) for i in every]
        size *= 2

    akv = [_rw_dot(a_ak[i], v2_[i]) for i in every]
    x = [_mm3(t[i], jnp.concatenate([a2_[i], akv[i]], axis=1)) for i in every]
    big = []
    for i in every:
        lhs = jnp.concatenate(
            [jnp.concatenate([a_rb[i], a_rk[i]], axis=1),
             jnp.concatenate([(b2_[i] * w_last[i]).T, (k2_[i] * w_last[i]).T], axis=1)], axis=0)
        rhs = jnp.concatenate([x[i], jnp.concatenate([zeros, v2_[i]], axis=1)], axis=0)
        big.append(_mm3(lhs, rhs))

    states = [state_ref[b] for b in range(bb)]
    for i, (b, c) in enumerate(items):
        r_hat = r2_[i] + big[i][:LANES, :LANES]
        st = _rw_dot(jnp.concatenate([r_hat, big[i][LANES:, :LANES]], axis=0), states[b])
        o2 = st[:LANES] + big[i][:LANES, LANES:]
        obuf_ref[b, c * L:(c + 1) * L, :] = o2[:L] + o2[L:]
        w_rows = jnp.broadcast_to(w_last[i], (LANES, LANES)).T
        states[b] = states[b] * w_rows + st[LANES:] + big[i][LANES:, LANES:]
    for b in range(bb):
        state_ref[b] = states[b]

    inv_n = 1.0 / RW_HEAD_DIM
    for b in range(bb):
        r, k, v, _, _, gate, _ = seqs[b]
        o = obuf_ref[b]
        mean = head_sum(o) * inv_n
        d = o - mean
        var = head_sum(d * d) * inv_n
        on = d * lax.rsqrt(var + RW_GN_EPS) * lng_ref[...] + lnb_ref[...]
        bonus = head_sum(r * k * rk_ref[...]) * v
        o_ref[b] = ((on + bonus) * gate).astype(o_ref.dtype)


def _rwkv(p_rw, mu, w0, w2, a0, a2, g2, k_k, k_a, r_k, lnx_g, lnx_b, *, batch, seq):
    width = w0.shape[-1]
    pairs = width // LANES
    tb = min(RW_BLOCK, seq)
    lo_w = RW_DECAY_LORA + RW_AAA_LORA + RW_GATE_LORA
    lo_blk = 3 * width // lo_w
    p3 = p_rw.reshape(batch, seq, p_rw.shape[-1])

    def col(off):
        return lambda h, j: (0, j, off + h)

    def vec(off):
        return lambda h, j: (0, off + h)

    row = lambda a: a.reshape(1, -1)
    kern = functools.partial(_rwkv_kernel, bb=batch, tb=tb)
    out = pl.pallas_call(
        kern,
        grid=(pairs, seq // tb),
        in_specs=[pl.BlockSpec((batch, tb, LANES), col(0)),
                  pl.BlockSpec((batch, tb, LANES), col(pairs)),
                  pl.BlockSpec((batch, tb, LANES), col(2 * pairs)),
                  pl.BlockSpec((batch, tb, lo_w), lambda h, j: (0, j, lo_blk)),
                  pl.BlockSpec((1, LANES), vec(0)),
                  pl.BlockSpec((1, LANES), vec(pairs)),
                  pl.BlockSpec((1, LANES), vec(2 * pairs)),
                  pl.BlockSpec((1, lo_w), lambda h, j: (0, lo_blk)),
                  pl.BlockSpec((1, LANES), vec(0)),
                  pl.BlockSpec((RW_DECAY_LORA, LANES), vec(0)),
                  pl.BlockSpec((1, LANES), vec(0)),
                  pl.BlockSpec((RW_AAA_LORA, LANES), vec(0)),
                  pl.BlockSpec((RW_GATE_LORA, LANES), vec(0)),
                  pl.BlockSpec((1, LANES), vec(0)),
                  pl.BlockSpec((1, LANES), vec(0)),
                  pl.BlockSpec((1, LANES), vec(0)),
                  pl.BlockSpec((1, LANES), vec(0)),
                  pl.BlockSpec((1, LANES), vec(0))],
        out_specs=pl.BlockSpec((batch, tb, LANES), lambda h, j: (0, j, h)),
        out_shape=jax.ShapeDtypeStruct((batch, seq, width), RW_OUT_DTYPE),
        scratch_shapes=[pltpu.VMEM((batch, LANES, LANES), F32),
                        pltpu.VMEM((batch, SUBLANES, LANES), F32),
                        pltpu.VMEM((batch, SUBLANES, LANES), F32),
                        pltpu.VMEM((batch, SUBLANES, LANES), F32),
                        pltpu.VMEM((batch, SUBLANES, lo_w), F32),
                        pltpu.VMEM((batch, tb, LANES), F32)],
        compiler_params=_cparams("parallel", "arbitrary"),
        name="rwkv",
    )(p3, p3, p3, p3, row(mu), row(mu), row(mu), row(mu), row(w0), w2, row(a0), a2, g2,
      row(k_k), row(k_a), row(r_k), row(lnx_g), row(lnx_b))
    return out.reshape(batch * seq, width)


def _rope_table_kernel(pos_ref, freq_ref, cos_ref, sa_ref, sb_ref):
    half = MLA_ROPE // 2
    ang = pos_ref[...].astype(F32) * freq_ref[...]
    lane = lax.broadcasted_iota(jnp.int32, ang.shape, 1)
    cos = jnp.cos(ang)
    sin = jnp.sin(ang)
    cos_ref[...] = jnp.where(lane < MLA_ROPE, cos, 1.0)
    sa_ref[...] = jnp.where(lane < half, -sin, 0.0)
    sb_ref[...] = jnp.where((lane >= half) & (lane < MLA_ROPE), sin, 0.0)


def _rope_tables(positions):
    t = positions.size
    half = MLA_ROPE // 2
    tm = min(2048, t)
    inv_freq = ROPE_THETA ** (-jnp.arange(half, dtype=F32) / half)
    freq = jnp.concatenate([inv_freq, inv_freq, jnp.zeros((LANES - MLA_ROPE,), F32)])
    spec = pl.BlockSpec((tm, LANES), lambda i: (i, 0))
    shp = jax.ShapeDtypeStruct((t, LANES), F32)
    return pl.pallas_call(
        _rope_table_kernel,
        grid=(t // tm,),
        in_specs=[pl.BlockSpec((tm, 1), lambda i: (i, 0)),
                  pl.BlockSpec((1, LANES), lambda i: (0, 0))],
        out_specs=[spec, spec, spec],
        out_shape=[shp, shp, shp],
        compiler_params=_cparams("parallel"),
        name="rope_tables",
    )(positions.reshape(t, 1), freq.reshape(1, LANES))


def _rope(x, cos, sa, sb):
    half = MLA_ROPE // 2
    return x * cos + pltpu.roll(x, LANES - half, axis=1) * sa + pltpu.roll(x, half, axis=1) * sb


def _mla_up_kernel(p_ref, cos_ref, sa_ref, sb_ref, qg_ref, wq_ref, kvg_ref, wk_ref, wvt_ref,
                   q_ref, k_ref, vt_ref, *, heads):
    p = p_ref[...]
    cos, sa, sb = cos_ref[...], sa_ref[...], sb_ref[...]

    def rms(x, g):
        return x * lax.rsqrt(jnp.mean(x * x, -1, keepdims=True) + RMS_EPS) * g

    q_lat = rms(p[:, :MLA_Q_LORA], qg_ref[...]).astype(BF16)
    kv_lat = rms(p[:, MLA_Q_LORA:MLA_Q_LORA + MLA_KV_LORA], kvg_ref[...]).astype(BF16)
    k_pe = p[:, MLA_Q_LORA + MLA_KV_LORA:]
    k_pe = jnp.concatenate([k_pe, jnp.zeros((k_pe.shape[0], LANES - MLA_ROPE), F32)], axis=1)
    k_pe = _rope(k_pe, cos, sa, sb).astype(BF16)

    q = _dot(q_lat, wq_ref[...]) * MLA_SCALE
    k_nope = _dot(kv_lat, wk_ref[...])
    vt_ref[...] = _dot_nt(wvt_ref[...], kv_lat).astype(BF16)
    for h in range(heads):
        base = h * MLA_QK_PAD
        q_ref[:, base:base + MLA_NOPE] = q[:, base:base + MLA_NOPE].astype(BF16)
        q_ref[:, base + MLA_NOPE:base + MLA_QK_PAD] = _rope(
            q[:, base + MLA_NOPE:base + MLA_QK_PAD], cos, sa, sb).astype(BF16)
        k_ref[:, base:base + MLA_NOPE] = k_nope[:, h * MLA_NOPE:(h + 1) * MLA_NOPE].astype(BF16)
        k_ref[:, base + MLA_NOPE:base + MLA_QK_PAD] = k_pe


def _mla_up(p_mla, tables, q_norm_g, w_uq, kv_norm_g, w_ukv, *, seq):
    t, cols = p_mla.shape
    heads = w_uq.shape[1] // (MLA_NOPE + MLA_ROPE)
    tm = min(TOKEN_TILE, seq)
    wq = w_uq.reshape(MLA_Q_LORA, heads, MLA_NOPE + MLA_ROPE)
    wq = jnp.pad(wq, ((0, 0), (0, 0), (0, MLA_QK_PAD - MLA_NOPE - MLA_ROPE)))
    wq = wq.reshape(MLA_Q_LORA, heads * MLA_QK_PAD).astype(BF16)
    wkv = w_ukv.reshape(MLA_KV_LORA, heads, MLA_NOPE + MLA_V)
    wk = wkv[:, :, :MLA_NOPE].reshape(MLA_KV_LORA, heads * MLA_NOPE).astype(BF16)
    wvt = wkv[:, :, MLA_NOPE:].reshape(MLA_KV_LORA, heads * MLA_V).T.astype(BF16)
    tok = lambda n: pl.BlockSpec((tm, n), lambda i: (i, 0))
    full = lambda a: pl.BlockSpec(a.shape, lambda i: (0, 0))
    qg = q_norm_g.reshape(1, -1)
    kvg = kv_norm_g.reshape(1, -1)
    kern = functools.partial(_mla_up_kernel, heads=heads)
    return pl.pallas_call(
        kern,
        grid=(t // tm,),
        in_specs=[tok(cols), tok(LANES), tok(LANES), tok(LANES),
                  full(qg), full(wq), full(kvg), full(wk), full(wvt)],
        out_specs=[tok(heads * MLA_QK_PAD), tok(heads * MLA_QK_PAD),
                   pl.BlockSpec((heads * MLA_V, tm), lambda i: (0, i))],
        out_shape=[jax.ShapeDtypeStruct((t, heads * MLA_QK_PAD), BF16),
                   jax.ShapeDtypeStruct((t, heads * MLA_QK_PAD), BF16),
                   jax.ShapeDtypeStruct((heads * MLA_V, t), BF16)],
        compiler_params=_cparams("parallel"),
        name="mla_up",
    )(p_mla, *tables, qg, wq, kvg, wk, wvt)


def _attn_kernel(qi_ref, kj_ref, q_ref, k_ref, vt_ref, o_ref, m_ref, l_ref, acc_ref, *, heads):
    step_id = pl.program_id(1)
    i = qi_ref[step_id]
    j = kj_ref[step_id]

    @pl.when(j == 0)
    def _():
        m_ref[...] = jnp.full_like(m_ref, -jnp.inf)
        l_ref[...] = jnp.zeros_like(l_ref)
        acc_ref[...] = jnp.zeros_like(acc_ref)

    def qk(h):
        return slice(h * MLA_QK_PAD, (h + 1) * MLA_QK_PAD)

    def vo(h):
        return slice(h * MLA_V, (h + 1) * MLA_V)

    def step(masked):
        hs = range(heads)
        s = [_dot_nt(k_ref[:, qk(h)], q_ref[:, qk(h)]) for h in hs]
        if masked:
            key = lax.broadcasted_iota(jnp.int32, s[0].shape, 0)
            query = lax.broadcasted_iota(jnp.int32, s[0].shape, 1)
            s = [jnp.where(key <= query, x, -jnp.inf) for x in s]
        m_prev = [m_ref[h] for h in hs]
        m_new = [jnp.maximum(m_prev[h], jnp.max(s[h], 0, keepdims=True)) for h in hs]
        alpha = [jnp.exp(m_prev[h] - m_new[h]) for h in hs]
        p = [jnp.exp(s[h] - m_new[h]) for h in hs]
        pv = [_dot(vt_ref[vo(h), :], p[h].astype(BF16)) for h in hs]
        for h in hs:
            l_ref[h] = alpha[h] * l_ref[h] + jnp.sum(p[h], 0, keepdims=True)
            acc_ref[vo(h), :] = alpha[h] * acc_ref[vo(h), :] + pv[h]
            m_ref[h] = m_new[h]

    @pl.when(j < i)
    def _():
        step(False)

    @pl.when(j == i)
    def _():
        step(True)
        for h in range(heads):
            o_ref[:, vo(h)] = (acc_ref[vo(h), :] / l_ref[h]).T.astype(o_ref.dtype)


def _attention(q, k, vt, *, batch, seq):
    t = q.shape[0]
    heads = vt.shape[0] // MLA_V
    tq = min(ATTN_TILE, seq)
    nq = seq // tq
    pairs = [(i, j) for i in range(nq) for j in range(i + 1)]
    qi = jnp.asarray([p[0] for p in pairs], jnp.int32)
    kj = jnp.asarray([p[1] for p in pairs], jnp.int32)
    kern = functools.partial(_attn_kernel, heads=heads)
    grid_spec = pltpu.PrefetchScalarGridSpec(
        num_scalar_prefetch=2,
        grid=(batch, len(pairs)),
        in_specs=[pl.BlockSpec((tq, heads * MLA_QK_PAD), lambda b, s, qi, kj: (b * nq + qi[s], 0)),
                  pl.BlockSpec((tq, heads * MLA_QK_PAD), lambda b, s, qi, kj: (b * nq + kj[s], 0)),
                  pl.BlockSpec((heads * MLA_V, tq), lambda b, s, qi, kj: (0, b * nq + kj[s]))],
        out_specs=pl.BlockSpec((tq, heads * MLA_V), lambda b, s, qi, kj: (b * nq + qi[s], 0)),
        scratch_shapes=[pltpu.VMEM((heads, 1, tq), F32), pltpu.VMEM((heads, 1, tq), F32),
                        pltpu.VMEM((heads * MLA_V, tq), F32)])
    return pl.pallas_call(
        kern,
        grid_spec=grid_spec,
        out_shape=jax.ShapeDtypeStruct((t, heads * MLA_V), BF16),
        compiler_params=_cparams("parallel", "arbitrary"),
        name="attention",
    )(qi, kj, q, k, vt)


def _conv_kernel(lin_ref, gate_ref, w_ref, b_ref, lng_ref, lnb_ref, o_ref, u_ref, *, tm):
    j = pl.program_id(1)

    @pl.when(j == 0)
    def _():
        u_ref[0:CONV_HALO, :] = jnp.zeros((CONV_HALO, u_ref.shape[1]), F32)

    u_ref[CONV_HALO:, :] = lin_ref[0] * _sigmoid(gate_ref[0])
    first = CONV_HALO - (CONV_K - 1)
    u = u_ref[...]
    rows = u.shape[0]
    acc = jnp.zeros((tm, u.shape[1]), F32)
    for phase in range(SUBLANES):
        shifted = u if phase == 0 else pltpu.roll(u, rows - phase, axis=0)
        for base in range(0, CONV_HALO + 1, SUBLANES):
            tap = base + phase - first
            if 0 <= tap < CONV_K:
                acc = acc + shifted[base:base + tm, :] * w_ref[tap:tap + 1, :]
    halo = u_ref[tm:tm + CONV_HALO, :]
    u_ref[0:CONV_HALO, :] = halo
    y = _layer_norm(acc + b_ref[...], lng_ref[...], lnb_ref[...])
    o_ref[0] = _silu(y).astype(o_ref.dtype)


def _conv(p_conv, conv_w, conv_b, ln_g, ln_b, *, batch, seq):
    ch = conv_w.shape[1]
    tm = min(TOKEN_TILE, seq)
    p3 = p_conv.reshape(batch, seq, 2 * ch)
    full = lambda a: pl.BlockSpec(a.shape, lambda b, j: (0, 0))
    row = lambda a: a.reshape(1, -1)
    kern = functools.partial(_conv_kernel, tm=tm)
    out = pl.pallas_call(
        kern,
        grid=(batch, seq // tm),
        in_specs=[pl.BlockSpec((1, tm, ch), lambda b, j: (b, j, 0)),
                  pl.BlockSpec((1, tm, ch), lambda b, j: (b, j, 1)),
                  full(conv_w), full(row(conv_b)), full(row(ln_g)), full(row(ln_b))],
        out_specs=pl.BlockSpec((1, tm, ch), lambda b, j: (b, j, 0)),
        out_shape=jax.ShapeDtypeStruct((batch, seq, ch), BF16),
        scratch_shapes=[pltpu.VMEM((CONV_HALO + tm, ch), F32)],
        compiler_params=_cparams("parallel", "arbitrary"),
        name="conv",
    )(p3, p3, conv_w, row(conv_b), row(ln_g), row(ln_b))
    return out.reshape(batch * seq, ch)


def _out_proj_kernel(x_ref, mod_ref, y1_ref, y2_ref, y3_ref, w1_ref, w2_ref, w3_ref,
                     lng_ref, lnb_ref, o_ref, *, mod_base, alpha):
    y = (_dot(y1_ref[...], w1_ref[0]) + _dot(y2_ref[...], w2_ref[0])
         + _dot(y3_ref[...], w3_ref[0]))
    g = mod_ref[0, mod_base + 2:mod_base + 3, :]
    o_ref[...] = _layer_norm(alpha * x_ref[...] + (1.0 + g) * y, lng_ref[...], lnb_ref[...])


def _out_proj(x, mod, ys, w_out, ln_g, ln_b, *, layer, mod_base, alpha, seq):
    t, d = x.shape
    tm = min(TOKEN_TILE, seq)
    per_b = seq // tm
    tok = lambda n: pl.BlockSpec((tm, n), lambda i: (i, 0))
    full = lambda a: pl.BlockSpec(a.shape, lambda i: (0, 0))
    w_specs = []
    offset = 0
    for y in ys:
        width = y.shape[1]
        assert offset % width == 0
        w_specs.append(pl.BlockSpec((1, width, d), functools.partial(
            lambda i, blk: (layer, blk, 0), blk=offset // width)))
        offset += width
    assert offset == w_out.shape[1]
    kern = functools.partial(_out_proj_kernel, mod_base=mod_base, alpha=alpha)
    return pl.pallas_call(
        kern,
        grid=(t // tm,),
        in_specs=[tok(d), pl.BlockSpec((1, N_MOD, d), lambda i: (i // per_b, 0, 0)),
                  *[tok(y.shape[1]) for y in ys], *w_specs, full(ln_g), full(ln_b)],
        out_specs=tok(d),
        out_shape=jax.ShapeDtypeStruct((t, d), F32),
        compiler_params=_cparams("parallel"),
        name="out_proj",
    )(x, mod, *ys, w_out, w_out, w_out, ln_g, ln_b)


def kernel(x, c, positions, w_ada, b_ada, ln_g, ln_b, w_ffn1_in, w_ffn1_out, w_ffn2_in, w_ffn2_out, w_in, w_out, rw_mu, rw_w0, rw_w2, rw_a0, rw_a2, rw_g2, rw_k_k, rw_k_a, rw_r_k, rw_lnx_g, rw_lnx_b, mla_q_norm_g, mla_w_uq, mla_kv_norm_g, mla_w_ukv, conv_w, conv_b, conv_ln_g, conv_ln_b):
    batch, seq, d = x.shape
    depth = w_ada.shape[0]
    alpha = (2 * depth) ** 0.25
    rw_cols = rw_mu.shape[1]
    mla_cols = MLA_Q_LORA + MLA_KV_LORA + MLA_ROPE
    conv_cols = w_in.shape[2] - rw_cols - mla_cols

    mod_all = _ada(c, w_ada, b_ada).reshape(depth, batch, N_MOD, d)
    tables = _rope_tables(positions)
    xt = x.reshape(batch * seq, d)
    row = lambda a: a.reshape(1, -1)
    w_ffn1_in, w_ffn1_out, w_ffn2_in, w_ffn2_out, w_in, w_out = (
        w.astype(BF16) for w in (w_ffn1_in, w_ffn1_out, w_ffn2_in, w_ffn2_out, w_in, w_out))

    for l in range(depth):
        mod = mod_all[l]
        xt = _ffn(xt, mod, w_ffn1_in, w_ffn1_out, row(ln_g[l, 0]), row(ln_b[l, 0]),
                  layer=l, mod_base=0, alpha=alpha, seq=seq)

        p_rw = _mod_matmul(xt, mod, w_in, layer=l, n=rw_cols, tn=rw_cols // 2,
                           mod_base=3, seq=seq)
        p_mla = _mod_matmul(xt, mod, w_in[l:l + 1, :, rw_cols:rw_cols + mla_cols], layer=0,
                            n=mla_cols, tn=mla_cols, mod_base=3, seq=seq)
        p_conv = _mod_matmul(xt, mod, w_in[l:l + 1, :, rw_cols + mla_cols:], layer=0,
                             n=conv_cols, tn=conv_cols, mod_base=3, seq=seq)

        y_rw = _rwkv(p_rw, rw_mu[l], rw_w0[l], rw_w2[l], rw_a0[l], rw_a2[l], rw_g2[l],
                     rw_k_k[l], rw_k_a[l], rw_r_k[l], rw_lnx_g[l], rw_lnx_b[l],
                     batch=batch, seq=seq)
        q, k, v = _mla_up(p_mla, tables, mla_q_norm_g[l], mla_w_uq[l], mla_kv_norm_g[l],
                          mla_w_ukv[l], seq=seq)
        y_mla = _attention(q, k, v, batch=batch, seq=seq)
        y_conv = _conv(p_conv, conv_w[l], conv_b[l], conv_ln_g[l], conv_ln_b[l],
                       batch=batch, seq=seq)

        xt = _out_proj(xt, mod, (y_rw, y_mla, y_conv), w_out, row(ln_g[l, 1]), row(ln_b[l, 1]),
                       layer=l, mod_base=3, alpha=alpha, seq=seq)

        xt = _ffn(xt, mod, w_ffn2_in, w_ffn2_out, row(ln_g[l, 2]), row(ln_b[l, 2]),
                  layer=l, mod_base=6, alpha=alpha, seq=seq)
    return xt.reshape(batch, seq, d)
```

```python
import functools

import jax
import jax.numpy as jnp
from jax import lax
from jax.experimental import pallas as pl
from jax.experimental.pallas import tpu as pltpu

F32 = jnp.float32
BF16 = jnp.bfloat16

LANES = 128
SUBLANES = 8
RW_HEAD_DIM = 64
RW_CHUNK = 64
RW_INV_BASE = 16
RW_OUT_DTYPE = jnp.bfloat16
RW_DECAY_LORA = 64
RW_AAA_LORA = 64
RW_GATE_LORA = 128
RW_GN_EPS = 64e-5
MLA_NOPE = 128
MLA_ROPE = 64
MLA_V = 128
MLA_Q_LORA = 384
MLA_KV_LORA = 256
MLA_QK_PAD = 256
MLA_SCALE = (MLA_NOPE + MLA_ROPE) ** -0.5
ROPE_THETA = 10000.0
CONV_K = 31
CONV_HALO = 32
N_MOD = 9
FFN_RES = 0.5
LN_EPS = 1e-5
RMS_EPS = 1e-6
VMEM_LIMIT = 52 * 1024 * 1024
TOKEN_TILE = 512
ATTN_TILE = 512
RW_BLOCK = 512
FFN_TILE = 512
FFN_TOKEN_TILE = 512


def _cparams(*sem):
    return pltpu.CompilerParams(dimension_semantics=sem, vmem_limit_bytes=VMEM_LIMIT)


def _dot(a, b):
    return jnp.dot(a, b, preferred_element_type=F32)


def _dot_nt(a, b):
    return lax.dot_general(a, b, (((1,), (1,)), ((), ())), preferred_element_type=F32)


def _rw_dot(a, b):
    return _dot(a.astype(BF16), b.astype(BF16))


def _split_bf16(x, terms):
    parts = []
    for _ in range(terms):
        hi = x.astype(BF16)
        parts.append(hi)
        x = x - hi.astype(F32)
    return parts


def _dot_split_lhs(x, exact_rhs, terms):
    return sum(_dot(part, exact_rhs) for part in _split_bf16(x, terms))


def _dot_split_rhs(exact_lhs, x, terms):
    return sum(_dot(exact_lhs, part) for part in _split_bf16(x, terms))


def _mm3(a, b):
    a_hi, a_lo = _split_bf16(a, 2)
    b_hi, b_lo = _split_bf16(b, 2)
    k, n = b.shape
    if k != LANES:
        return _dot(a_hi, b_hi) + _dot(a_lo, b_hi) + _dot(a_hi, b_lo)
    lhs = jnp.concatenate([a_hi, a_lo], axis=1)
    if n != LANES:
        return _dot(lhs, jnp.concatenate([b_hi, b_hi], axis=0)) + _dot(a_hi, b_lo)
    rhs = jnp.concatenate([jnp.concatenate([b_hi, b_lo], axis=1),
                           jnp.concatenate([b_hi, jnp.zeros_like(b_lo)], axis=1)], axis=0)
    out = _dot(lhs, rhs)
    return out[:, :n] + out[:, n:]


def _sigmoid(x):
    return 1.0 / (1.0 + jnp.exp(-x))


def _silu(x):
    return x * _sigmoid(x)


def _layer_norm(y, g, b):
    mean = jnp.mean(y, -1, keepdims=True)
    d = y - mean
    var = jnp.mean(d * d, -1, keepdims=True)
    return d * lax.rsqrt(var + LN_EPS) * g + b


def _ada_kernel(ct_ref, w_ref, b_ref, o_ref, *, batch):
    w = w_ref[0]
    rows = [jnp.sum(_silu(ct_ref[:, b:b + 1]) * w, axis=0, keepdims=True) for b in range(batch)]
    o_ref[0] = jnp.concatenate(rows, axis=0) + b_ref[0]


def _ada(c, w_ada, b_ada):
    depth, d, n = w_ada.shape
    batch = c.shape[0]
    tn = 1024
    kern = functools.partial(_ada_kernel, batch=batch)
    return pl.pallas_call(
        kern,
        grid=(depth, n // tn),
        in_specs=[pl.BlockSpec((d, batch), lambda l, j: (0, 0)),
                  pl.BlockSpec((1, d, tn), lambda l, j: (l, 0, j)),
                  pl.BlockSpec((1, 1, tn), lambda l, j: (l, 0, j))],
        out_specs=pl.BlockSpec((1, batch, tn), lambda l, j: (l, 0, j)),
        out_shape=jax.ShapeDtypeStruct((depth, batch, n), F32),
        compiler_params=_cparams("parallel", "parallel"),
        name="ada",
    )(c.T, w_ada, b_ada.reshape(depth, 1, n))


def _ffn_kernel(x_ref, mod_ref, wg_ref, wu_ref, wo_ref, lng_ref, lnb_ref, o_ref,
                h_ref, *, mod_base, alpha):
    j = pl.program_id(1)

    @pl.when(j == 0)
    def _():
        sh = mod_ref[0, mod_base:mod_base + 1, :]
        sc = mod_ref[0, mod_base + 1:mod_base + 2, :]
        h_ref[...] = (x_ref[...] * (1.0 + sc) + sh).astype(BF16)
        o_ref[...] = jnp.zeros_like(o_ref)

    h = h_ref[...]
    gate = _dot(h, wg_ref[0])
    up = _dot(h, wu_ref[0])
    act = (_silu(gate) * up).astype(BF16)
    o_ref[...] += _dot(act, wo_ref[0])

    @pl.when(j == pl.num_programs(1) - 1)
    def _():
        g = mod_ref[0, mod_base + 2:mod_base + 3, :]
        y = alpha * x_ref[...] + (FFN_RES * (1.0 + g)) * o_ref[...]
        o_ref[...] = _layer_norm(y, lng_ref[...], lnb_ref[...])


def _ffn(x, mod, w_in, w_out, ln_g, ln_b, *, layer, mod_base, alpha, seq):
    t, d = x.shape
    f = w_out.shape[1]
    tm = min(FFN_TOKEN_TILE, seq)
    tf = FFN_TILE
    nf = f // tf
    per_b = seq // tm
    kern = functools.partial(_ffn_kernel, mod_base=mod_base, alpha=alpha)
    return pl.pallas_call(
        kern,
        grid=(t // tm, nf),
        in_specs=[pl.BlockSpec((tm, d), lambda i, j: (i, 0)),
                  pl.BlockSpec((1, N_MOD, d), lambda i, j: (i // per_b, 0, 0)),
                  pl.BlockSpec((1, d, tf), lambda i, j: (layer, 0, j)),
                  pl.BlockSpec((1, d, tf), lambda i, j: (layer, 0, nf + j)),
                  pl.BlockSpec((1, tf, d), lambda i, j: (layer, j, 0)),
                  pl.BlockSpec((1, d), lambda i, j: (0, 0)),
                  pl.BlockSpec((1, d), lambda i, j: (0, 0))],
        out_specs=pl.BlockSpec((tm, d), lambda i, j: (i, 0)),
        out_shape=jax.ShapeDtypeStruct((t, d), F32),
        scratch_shapes=[pltpu.VMEM((tm, d), BF16)],
        compiler_params=_cparams("parallel", "arbitrary"),
        name="ffn",
    )(x, mod, w_in, w_in, w_out, ln_g, ln_b)


def _in_proj_kernel(x_ref, mod_ref, w_rw_ref, w_mla_ref, w_conv_ref,
                    rw_ref, mla_ref, conv_ref, *, mod_base):
    sh = mod_ref[0, mod_base:mod_base + 1, :]
    sc = mod_ref[0, mod_base + 1:mod_base + 2, :]
    h = (x_ref[...] * (1.0 + sc) + sh).astype(BF16)
    rw_ref[...] = _dot(h, w_rw_ref[0])
    mla_ref[...] = _dot(h, w_mla_ref[0])
    conv_ref[...] = _dot(h, w_conv_ref[0])


def _in_proj(x, mod, w_in, *, layer, cols, mod_base, seq):
    t, d = x.shape
    rw_cols, mla_cols, conv_cols = cols
    tm = min(TOKEN_TILE, seq)
    per_b = seq // tm
    w_mla = w_in[layer:layer + 1, :, rw_cols:rw_cols + mla_cols]
    w_conv = w_in[layer:layer + 1, :, rw_cols + mla_cols:]
    tok = lambda n: pl.BlockSpec((tm, n), lambda i: (i, 0))
    resident = lambda n, l: pl.BlockSpec((1, d, n), lambda i: (l, 0, 0),
                                         pipeline_mode=pl.Buffered(1))
    kern = functools.partial(_in_proj_kernel, mod_base=mod_base)
    return pl.pallas_call(
        kern,
        grid=(t // tm,),
        in_specs=[tok(d), pl.BlockSpec((1, N_MOD, d), lambda i: (i // per_b, 0, 0)),
                  resident(rw_cols, layer), resident(mla_cols, 0), resident(conv_cols, 0)],
        out_specs=[tok(rw_cols), tok(mla_cols), tok(conv_cols)],
        out_shape=[jax.ShapeDtypeStruct((t, n), F32) for n in cols],
        compiler_params=_cparams("parallel"),
        name="in_proj",
    )(x, mod, w_in, w_mla, w_conv)


def _rwkv_kernel(r_ref, k_ref, v_ref, lo_ref, mur_ref, muk_ref, muv_ref, mulo_ref,
                 w0_ref, w2_ref, a0_ref, a2_ref, g2_ref, kk_ref, ka_ref, rk_ref,
                 lng_ref, lnb_ref, o_ref,
                 state_ref, pr_ref, pk_ref, pv_ref, plo_ref, obuf_ref, *, bb, tb):
    j = pl.program_id(1)
    L = RW_CHUNK
    nc = tb // L

    @pl.when(j == 0)
    def _():
        state_ref[...] = jnp.zeros_like(state_ref)
        pr_ref[...] = jnp.zeros_like(pr_ref)
        pk_ref[...] = jnp.zeros_like(pk_ref)
        pv_ref[...] = jnp.zeros_like(pv_ref)
        plo_ref[...] = jnp.zeros_like(plo_ref)

    lane_r = lax.broadcasted_iota(jnp.int32, (LANES, LANES), 0)
    lane_c = lax.broadcasted_iota(jnp.int32, (LANES, LANES), 1)

    def same_block(size):
        return (lane_r // size) == (lane_c // size)

    same_head = same_block(RW_HEAD_DIM).astype(BF16)
    stack_mask = ((lane_r // L) == (lane_c // RW_HEAD_DIM)).astype(F32)
    strict_lower = lane_r > lane_c
    lower = lane_r >= lane_c
    eye = (lane_r == lane_c).astype(F32)
    tri = (lax.broadcasted_iota(jnp.int32, (L, L), 0)
           >= lax.broadcasted_iota(jnp.int32, (L, L), 1)).astype(BF16)
    zeros = jnp.zeros((LANES, LANES), F32)

    def head_sum(x):
        return _dot_split_lhs(x, same_head, 2)

    def stack(x):
        return jnp.concatenate([x, x], axis=0) * stack_mask

    def prologue(b):
        def shift_mix(p_ref, prev_ref, mu_ref):
            p = p_ref[b]
            row = lax.broadcasted_iota(jnp.int32, p.shape, 0)
            prev = jnp.where(row == 0, prev_ref[b, 0:1, :], pltpu.roll(p, 1, axis=0))
            prev_ref[b, 0:1, :] = p[tb - 1:tb, :]
            return p + (prev - p) * mu_ref[...]

        r = shift_mix(r_ref, pr_ref, mur_ref)
        k = shift_mix(k_ref, pk_ref, muk_ref)
        v = shift_mix(v_ref, pv_ref, muv_ref)
        lo = shift_mix(lo_ref, plo_ref, mulo_ref)
        w_lo = lo[:, :RW_DECAY_LORA]
        a_lo = lo[:, RW_DECAY_LORA:RW_DECAY_LORA + RW_AAA_LORA]
        g_lo = lo[:, RW_DECAY_LORA + RW_AAA_LORA:]
        z = w0_ref[...] + _rw_dot(jnp.tanh(w_lo), w2_ref[...])
        softplus_neg_z = jnp.maximum(-z, 0.0) + jnp.log(1.0 + jnp.exp(-jnp.abs(z)))
        log_decay = -jnp.exp(-softplus_neg_z - 0.5)
        a = _sigmoid(a0_ref[...] + _rw_dot(a_lo, a2_ref[...]))
        gate = _rw_dot(_sigmoid(g_lo), g2_ref[...])
        kk = k * kk_ref[...]
        kk = kk / jnp.maximum(jnp.sqrt(head_sum(kk * kk)), 1e-12)
        k = k * (1.0 + (a - 1.0) * ka_ref[...])
        return r, k, v, kk, a, gate, log_decay

    seqs = [prologue(b) for b in range(bb)]
    items = [(b, c) for c in range(nc) for b in range(bb)]
    every = range(len(items))

    def chunk_operands(b, c):
        r, k, v, kk, a, _, log_decay = seqs[b]
        rows = slice(c * L, (c + 1) * L)
        lw = log_decay[rows]
        cum = _dot_split_rhs(tri, lw, 3)
        w_cum = jnp.exp(cum)
        w_inv = jnp.exp(-cum)
        w_prev = jnp.exp(cum - lw)
        w_last = w_cum[L - 1:L, :]
        kk_c = kk[rows]
        return (stack(-kk_c * w_prev), stack(kk_c * a[rows] * w_inv), stack(k[rows] * w_inv),
                stack(r[rows] * w_cum), stack(v[rows]), w_last)

    ops = [chunk_operands(b, c) for b, c in items]
    a2_ = [o[0] for o in ops]
    b2_ = [o[1] for o in ops]
    k2_ = [o[2] for o in ops]
    r2_ = [o[3] for o in ops]
    v2_ = [o[4] for o in ops]
    w_last = [o[5] for o in ops]

    scores = [_dot_nt(jnp.concatenate([a2_[i], r2_[i]], axis=0).astype(BF16),
                      jnp.concatenate([b2_[i], k2_[i]], axis=0).astype(BF16)) for i in every]
    a_ab = [jnp.where(strict_lower, s[:LANES, :LANES], 0.0) for s in scores]
    a_ak = [jnp.where(strict_lower, s[:LANES, LANES:], 0.0) for s in scores]
    a_rb = [jnp.where(lower, s[LANES:, :LANES], 0.0) for s in scores]
    a_rk = [jnp.where(lower, s[LANES:, LANES:], 0.0) for s in scores]

    diag_blocks = same_block(RW_INV_BASE)
    p = [jnp.where(diag_blocks, x, 0.0) for x in a_ab]
    t = [eye + x for x in p]
    n = 2
    while n < RW_INV_BASE:
        p = [_mm3(x, x) for x in p]
        t = [t[i] + _mm3(t[i], p[i]) for i in every]
        n *= 2
    size = 2 * RW_INV_BASE
    while size <= L:
        off_diag = same_block(size) & ~same_block(size // 2)
        et = [_mm3(jnp.where(off_diag, a_ab[i], 0.0), t[i]) for i in every]
        t = [t[i] + _mm3(t[i], et[i]) for i in every]
        size *= 2

    akv = [_rw_dot(a_ak[i], v2_[i]) for i in every]
    x = [_mm3(t[i], jnp.concatenate([a2_[i], akv[i]], axis=1)) for i in every]
    big = []
    for i in every:
        lhs = jnp.concatenate(
            [jnp.concatenate([a_rb[i], a_rk[i]], axis=1),
             jnp.concatenate([(b2_[i] * w_last[i]).T, (k2_[i] * w_last[i]).T], axis=1)], axis=0)
        rhs = jnp.concatenate([x[i], jnp.concatenate([zeros, v2_[i]], axis=1)], axis=0)
        big.append(_mm3(lhs, rhs))

    states = [state_ref[b] for b in range(bb)]
    for i, (b, c) in enumerate(items):
        r_hat = r2_[i] + big[i][:LANES, :LANES]
        st = _rw_dot(jnp.concatenate([r_hat, big[i][LANES:, :LANES]], axis=0), states[b])
        o2 = st[:LANES] + big[i][:LANES, LANES:]
        obuf_ref[b, c * L:(c + 1) * L, :] = o2[:L] + o2[L:]
        w_rows = jnp.broadcast_to(w_last[i], (LANES, LANES)).T
        states[b] = states[b] * w_rows + st[LANES:] + big[i][LANES:, LANES:]
    for b in range(bb):
        state_ref[b] = states[b]

    inv_n = 1.0 / RW_HEAD_DIM
    for b in range(bb):
        r, k, v, _, _, gate, _ = seqs[b]
        o = obuf_ref[b]
        mean = head_sum(o) * inv_n
        d = o - mean
        var = head_sum(d * d) * inv_n
        on = d * lax.rsqrt(var + RW_GN_EPS) * lng_ref[...] + lnb_ref[...]
        bonus = head_sum(r * k * rk_ref[...]) * v
        o_ref[b] = ((on + bonus) * gate).astype(o_ref.dtype)


def _rwkv(p_rw, mu, w0, w2, a0, a2, g2, k_k, k_a, r_k, lnx_g, lnx_b, *, batch, seq):
    width = w0.shape[-1]
    pairs = width // LANES
    tb = min(RW_BLOCK, seq)
    lo_w = RW_DECAY_LORA + RW_AAA_LORA + RW_GATE_LORA
    lo_blk = 3 * width // lo_w
    p3 = p_rw.reshape(batch, seq, p_rw.shape[-1])

    def col(off):
        return lambda h, j: (0, j, off + h)

    def vec(off):
        return lambda h, j: (0, off + h)

    row = lambda a: a.reshape(1, -1)
    kern = functools.partial(_rwkv_kernel, bb=batch, tb=tb)
    out = pl.pallas_call(
        kern,
        grid=(pairs, seq // tb),
        in_specs=[pl.BlockSpec((batch, tb, LANES), col(0)),
                  pl.BlockSpec((batch, tb, LANES), col(pairs)),
                  pl.BlockSpec((batch, tb, LANES), col(2 * pairs)),
                  pl.BlockSpec((batch, tb, lo_w), lambda h, j: (0, j, lo_blk)),
                  pl.BlockSpec((1, LANES), vec(0)),
                  pl.BlockSpec((1, LANES), vec(pairs)),
                  pl.BlockSpec((1, LANES), vec(2 * pairs)),
                  pl.BlockSpec((1, lo_w), lambda h, j: (0, lo_blk)),
                  pl.BlockSpec((1, LANES), vec(0)),
                  pl.BlockSpec((RW_DECAY_LORA, LANES), vec(0)),
                  pl.BlockSpec((1, LANES), vec(0)),
                  pl.BlockSpec((RW_AAA_LORA, LANES), vec(0)),
                  pl.BlockSpec((RW_GATE_LORA, LANES), vec(0)),
                  pl.BlockSpec((1, LANES), vec(0)),
                  pl.BlockSpec((1, LANES), vec(0)),
                  pl.BlockSpec((1, LANES), vec(0)),
                  pl.BlockSpec((1, LANES), vec(0)),
                  pl.BlockSpec((1, LANES), vec(0))],
        out_specs=pl.BlockSpec((batch, tb, LANES), lambda h, j: (0, j, h)),
        out_shape=jax.ShapeDtypeStruct((batch, seq, width), RW_OUT_DTYPE),
        scratch_shapes=[pltpu.VMEM((batch, LANES, LANES), F32),
                        pltpu.VMEM((batch, SUBLANES, LANES), F32),
                        pltpu.VMEM((batch, SUBLANES, LANES), F32),
                        pltpu.VMEM((batch, SUBLANES, LANES), F32),
                        pltpu.VMEM((batch, SUBLANES, lo_w), F32),
                        pltpu.VMEM((batch, tb, LANES), F32)],
        compiler_params=_cparams("parallel", "arbitrary"),
        name="rwkv",
    )(p3, p3, p3, p3, row(mu), row(mu), row(mu), row(mu), row(w0), w2, row(a0), a2, g2,
      row(k_k), row(k_a), row(r_k), row(lnx_g), row(lnx_b))
    return out.reshape(batch * seq, width)


def _rope_table_kernel(pos_ref, freq_ref, cos_ref, sa_ref, sb_ref):
    half = MLA_ROPE // 2
    ang = pos_ref[...].astype(F32) * freq_ref[...]
    lane = lax.broadcasted_iota(jnp.int32, ang.shape, 1)
    cos = jnp.cos(ang)
    sin = jnp.sin(ang)
    cos_ref[...] = jnp.where(lane < MLA_ROPE, cos, 1.0)
    sa_ref[...] = jnp.where(lane < half, -sin, 0.0)
    sb_ref[...] = jnp.where((lane >= half) & (lane < MLA_ROPE), sin, 0.0)


def _rope_tables(positions):
    t = positions.size
    half = MLA_ROPE // 2
    tm = min(2048, t)
    inv_freq = ROPE_THETA ** (-jnp.arange(half, dtype=F32) / half)
    freq = jnp.concatenate([inv_freq, inv_freq, jnp.zeros((LANES - MLA_ROPE,), F32)])
    spec = pl.BlockSpec((tm, LANES), lambda i: (i, 0))
    shp = jax.ShapeDtypeStruct((t, LANES), F32)
    return pl.pallas_call(
        _rope_table_kernel,
        grid=(t // tm,),
        in_specs=[pl.BlockSpec((tm, 1), lambda i: (i, 0)),
                  pl.BlockSpec((1, LANES), lambda i: (0, 0))],
        out_specs=[spec, spec, spec],
        out_shape=[shp, shp, shp],
        compiler_params=_cparams("parallel"),
        name="rope_tables",
    )(positions.reshape(t, 1), freq.reshape(1, LANES))


def _rope(x, cos, sa, sb):
    half = MLA_ROPE // 2
    return x * cos + pltpu.roll(x, LANES - half, axis=1) * sa + pltpu.roll(x, half, axis=1) * sb


def _mla_up_kernel(p_ref, cos_ref, sa_ref, sb_ref, qg_ref, wq_ref, kvg_ref, wk_ref, wvt_ref,
                   q_ref, k_ref, vt_ref, *, heads):
    p = p_ref[...]
    cos, sa, sb = cos_ref[...], sa_ref[...], sb_ref[...]

    def rms(x, g):
        return x * lax.rsqrt(jnp.mean(x * x, -1, keepdims=True) + RMS_EPS) * g

    q_lat = rms(p[:, :MLA_Q_LORA], qg_ref[...]).astype(BF16)
    kv_lat = rms(p[:, MLA_Q_LORA:MLA_Q_LORA + MLA_KV_LORA], kvg_ref[...]).astype(BF16)
    k_pe = p[:, MLA_Q_LORA + MLA_KV_LORA:]
    k_pe = jnp.concatenate([k_pe, jnp.zeros((k_pe.shape[0], LANES - MLA_ROPE), F32)], axis=1)
    k_pe = _rope(k_pe, cos, sa, sb).astype(BF16)

    q = _dot(q_lat, wq_ref[...]) * MLA_SCALE
    k_nope = _dot(kv_lat, wk_ref[...])
    vt_ref[0] = _dot_nt(wvt_ref[...], kv_lat).astype(BF16)
    for h in range(heads):
        base = h * MLA_QK_PAD
        q_ref[:, base:base + MLA_NOPE] = q[:, base:base + MLA_NOPE].astype(BF16)
        q_ref[:, base + MLA_NOPE:base + MLA_QK_PAD] = _rope(
            q[:, base + MLA_NOPE:base + MLA_QK_PAD], cos, sa, sb).astype(BF16)
        k_ref[:, base:base + MLA_NOPE] = k_nope[:, h * MLA_NOPE:(h + 1) * MLA_NOPE].astype(BF16)
        k_ref[:, base + MLA_NOPE:base + MLA_QK_PAD] = k_pe


def _mla_up(p_mla, tables, q_norm_g, w_uq, kv_norm_g, w_ukv, *, seq):
    t, cols = p_mla.shape
    heads = w_uq.shape[1] // (MLA_NOPE + MLA_ROPE)
    tm = min(TOKEN_TILE, seq)
    wq = w_uq.reshape(MLA_Q_LORA, heads, MLA_NOPE + MLA_ROPE)
    wq = jnp.pad(wq, ((0, 0), (0, 0), (0, MLA_QK_PAD - MLA_NOPE - MLA_ROPE)))
    wq = wq.reshape(MLA_Q_LORA, heads * MLA_QK_PAD).astype(BF16)
    wkv = w_ukv.reshape(MLA_KV_LORA, heads, MLA_NOPE + MLA_V)
    wk = wkv[:, :, :MLA_NOPE].reshape(MLA_KV_LORA, heads * MLA_NOPE).astype(BF16)
    wvt = wkv[:, :, MLA_NOPE:].reshape(MLA_KV_LORA, heads * MLA_V).T.astype(BF16)
    tok = lambda n: pl.BlockSpec((tm, n), lambda i: (i, 0))
    full = lambda a: pl.BlockSpec(a.shape, lambda i: (0, 0))
    qg = q_norm_g.reshape(1, -1)
    kvg = kv_norm_g.reshape(1, -1)
    kern = functools.partial(_mla_up_kernel, heads=heads)
    return pl.pallas_call(
        kern,
        grid=(t // tm,),
        in_specs=[tok(cols), tok(LANES), tok(LANES), tok(LANES),
                  full(qg), full(wq), full(kvg), full(wk), full(wvt)],
        out_specs=[tok(heads * MLA_QK_PAD), tok(heads * MLA_QK_PAD),
                   pl.BlockSpec((1, heads * MLA_V, tm), lambda i: (i, 0, 0))],
        out_shape=[jax.ShapeDtypeStruct((t, heads * MLA_QK_PAD), BF16),
                   jax.ShapeDtypeStruct((t, heads * MLA_QK_PAD), BF16),
                   jax.ShapeDtypeStruct((t // tm, heads * MLA_V, tm), BF16)],
        compiler_params=_cparams("parallel"),
        name="mla_up",
    )(p_mla, *tables, qg, wq, kvg, wk, wvt)


def _attn_kernel(qi_ref, kj_ref, q_ref, k_ref, vt_ref, o_ref, m_ref, l_ref, acc_ref, *, heads):
    step_id = pl.program_id(1)
    i = qi_ref[step_id]
    j = kj_ref[step_id]

    @pl.when(j == 0)
    def _():
        m_ref[...] = jnp.full_like(m_ref, -jnp.inf)
        l_ref[...] = jnp.zeros_like(l_ref)
        acc_ref[...] = jnp.zeros_like(acc_ref)

    def qk(h):
        return slice(h * MLA_QK_PAD, (h + 1) * MLA_QK_PAD)

    def vo(h):
        return slice(h * MLA_V, (h + 1) * MLA_V)

    def step(masked):
        hs = range(heads)
        s = [_dot_nt(k_ref[:, qk(h)], q_ref[:, qk(h)]) for h in hs]
        if masked:
            key = lax.broadcasted_iota(jnp.int32, s[0].shape, 0)
            query = lax.broadcasted_iota(jnp.int32, s[0].shape, 1)
            s = [jnp.where(key <= query, x, -jnp.inf) for x in s]
        m_prev = [m_ref[h] for h in hs]
        m_new = [jnp.maximum(m_prev[h], jnp.max(s[h], 0, keepdims=True)) for h in hs]
        alpha = [jnp.exp(m_prev[h] - m_new[h]) for h in hs]
        p = [jnp.exp(s[h] - m_new[h]) for h in hs]
        pv = [_dot(vt_ref[0, vo(h), :], p[h].astype(BF16)) for h in hs]
        for h in hs:
            l_ref[h] = alpha[h] * l_ref[h] + jnp.sum(p[h], 0, keepdims=True)
            acc_ref[vo(h), :] = alpha[h] * acc_ref[vo(h), :] + pv[h]
            m_ref[h] = m_new[h]

    @pl.when(j < i)
    def _():
        step(False)

    @pl.when(j == i)
    def _():
        step(True)
        for h in range(heads):
            o_ref[:, vo(h)] = (acc_ref[vo(h), :] / l_ref[h]).T.astype(o_ref.dtype)


def _attention(q, k, vt, *, batch, seq):
    t = q.shape[0]
    heads = vt.shape[1] // MLA_V
    tq = min(ATTN_TILE, seq)
    assert vt.shape[2] == tq
    nq = seq // tq
    pairs = [(i, j) for i in range(nq) for j in range(i + 1)]
    qi = jnp.asarray([p[0] for p in pairs], jnp.int32)
    kj = jnp.asarray([p[1] for p in pairs], jnp.int32)
    kern = functools.partial(_attn_kernel, heads=heads)
    grid_spec = pltpu.PrefetchScalarGridSpec(
        num_scalar_prefetch=2,
        grid=(batch, len(pairs)),
        in_specs=[pl.BlockSpec((tq, heads * MLA_QK_PAD), lambda b, s, qi, kj: (b * nq + qi[s], 0)),
                  pl.BlockSpec((tq, heads * MLA_QK_PAD), lambda b, s, qi, kj: (b * nq + kj[s], 0)),
                  pl.BlockSpec((1, heads * MLA_V, tq),
                               lambda b, s, qi, kj: (b * nq + kj[s], 0, 0))],
        out_specs=pl.BlockSpec((tq, heads * MLA_V), lambda b, s, qi, kj: (b * nq + qi[s], 0)),
        scratch_shapes=[pltpu.VMEM((heads, 1, tq), F32), pltpu.VMEM((heads, 1, tq), F32),
                        pltpu.VMEM((heads * MLA_V, tq), F32)])
    return pl.pallas_call(
        kern,
        grid_spec=grid_spec,
        out_shape=jax.ShapeDtypeStruct((t, heads * MLA_V), BF16),
        compiler_params=_cparams("parallel", "arbitrary"),
        name="attention",
    )(qi, kj, q, k, vt)


def _conv_kernel(lin_ref, gate_ref, w_ref, b_ref, lng_ref, lnb_ref, o_ref, u_ref, *, tm):
    j = pl.program_id(1)

    @pl.when(j == 0)
    def _():
        u_ref[0:CONV_HALO, :] = jnp.zeros((CONV_HALO, u_ref.shape[1]), F32)

    u_ref[CONV_HALO:, :] = lin_ref[0] * _sigmoid(gate_ref[0])
    first = CONV_HALO - (CONV_K - 1)
    u = u_ref[...]
    rows = u.shape[0]
    acc = jnp.zeros((tm, u.shape[1]), F32)
    for phase in range(SUBLANES):
        shifted = u if phase == 0 else pltpu.roll(u, rows - phase, axis=0)
        for base in range(0, CONV_HALO + 1, SUBLANES):
            tap = base + phase - first
            if 0 <= tap < CONV_K:
                acc = acc + shifted[base:base + tm, :] * w_ref[tap:tap + 1, :]
    halo = u_ref[tm:tm + CONV_HALO, :]
    u_ref[0:CONV_HALO, :] = halo
    y = _layer_norm(acc + b_ref[...], lng_ref[...], lnb_ref[...])
    o_ref[0] = _silu(y).astype(o_ref.dtype)


def _conv(p_conv, conv_w, conv_b, ln_g, ln_b, *, batch, seq):
    ch = conv_w.shape[1]
    tm = min(TOKEN_TILE, seq)
    p3 = p_conv.reshape(batch, seq, 2 * ch)
    full = lambda a: pl.BlockSpec(a.shape, lambda b, j: (0, 0))
    row = lambda a: a.reshape(1, -1)
    kern = functools.partial(_conv_kernel, tm=tm)
    out = pl.pallas_call(
        kern,
        grid=(batch, seq // tm),
        in_specs=[pl.BlockSpec((1, tm, ch), lambda b, j: (b, j, 0)),
                  pl.BlockSpec((1, tm, ch), lambda b, j: (b, j, 1)),
                  full(conv_w), full(row(conv_b)), full(row(ln_g)), full(row(ln_b))],
        out_specs=pl.BlockSpec((1, tm, ch), lambda b, j: (b, j, 0)),
        out_shape=jax.ShapeDtypeStruct((batch, seq, ch), BF16),
        scratch_shapes=[pltpu.VMEM((CONV_HALO + tm, ch), F32)],
        compiler_params=_cparams("parallel", "arbitrary"),
        name="conv",
    )(p3, p3, conv_w, row(conv_b), row(ln_g), row(ln_b))
    return out.reshape(batch * seq, ch)


def _out_proj_kernel(x_ref, mod_ref, y1_ref, y2_ref, y3_ref, w1_ref, w2_ref, w3_ref,
                     lng_ref, lnb_ref, o_ref, *, mod_base, alpha):
    y = (_dot(y1_ref[...], w1_ref[0]) + _dot(y2_ref[...], w2_ref[0])
         + _dot(y3_ref[...], w3_ref[0]))
    g = mod_ref[0, mod_base + 2:mod_base + 3, :]
    o_ref[...] = _layer_norm(alpha * x_ref[...] + (1.0 + g) * y, lng_ref[...], lnb_ref[...])


def _out_proj(x, mod, ys, w_out, ln_g, ln_b, *, layer, mod_base, alpha, seq):
    t, d = x.shape
    tm = min(TOKEN_TILE, seq)
    per_b = seq // tm
    tok = lambda n: pl.BlockSpec((tm, n), lambda i: (i, 0))
    full = lambda a: pl.BlockSpec(a.shape, lambda i: (0, 0))
    w_specs = []
    offset = 0
    for y in ys:
        width = y.shape[1]
        assert offset % width == 0
        w_specs.append(pl.BlockSpec((1, width, d), functools.partial(
            lambda i, blk: (layer, blk, 0), blk=offset // width)))
        offset += width
    assert offset == w_out.shape[1]
    kern = functools.partial(_out_proj_kernel, mod_base=mod_base, alpha=alpha)
    return pl.pallas_call(
        kern,
        grid=(t // tm,),
        in_specs=[tok(d), pl.BlockSpec((1, N_MOD, d), lambda i: (i // per_b, 0, 0)),
                  *[tok(y.shape[1]) for y in ys], *w_specs, full(ln_g), full(ln_b)],
        out_specs=tok(d),
        out_shape=jax.ShapeDtypeStruct((t, d), F32),
        compiler_params=_cparams("parallel"),
        name="out_proj",
    )(x, mod, *ys, w_out, w_out, w_out, ln_g, ln_b)


def kernel(x, c, positions, w_ada, b_ada, ln_g, ln_b, w_ffn1_in, w_ffn1_out, w_ffn2_in, w_ffn2_out, w_in, w_out, rw_mu, rw_w0, rw_w2, rw_a0, rw_a2, rw_g2, rw_k_k, rw_k_a, rw_r_k, rw_lnx_g, rw_lnx_b, mla_q_norm_g, mla_w_uq, mla_kv_norm_g, mla_w_ukv, conv_w, conv_b, conv_ln_g, conv_ln_b):
    batch, seq, d = x.shape
    depth = w_ada.shape[0]
    alpha = (2 * depth) ** 0.25
    rw_cols = rw_mu.shape[1]
    mla_cols = MLA_Q_LORA + MLA_KV_LORA + MLA_ROPE
    conv_cols = w_in.shape[2] - rw_cols - mla_cols

    mod_all = _ada(c, w_ada, b_ada).reshape(depth, batch, N_MOD, d)
    tables = _rope_tables(positions)
    xt = x.reshape(batch * seq, d)
    row = lambda a: a.reshape(1, -1)
    w_ffn1_in, w_ffn1_out, w_ffn2_in, w_ffn2_out, w_in, w_out = (
        w.astype(BF16) for w in (w_ffn1_in, w_ffn1_out, w_ffn2_in, w_ffn2_out, w_in, w_out))

    for l in range(depth):
        mod = mod_all[l]
        xt = _ffn(xt, mod, w_ffn1_in, w_ffn1_out, row(ln_g[l, 0]), row(ln_b[l, 0]),
                  layer=l, mod_base=0, alpha=alpha, seq=seq)

        p_rw, p_mla, p_conv = _in_proj(xt, mod, w_in, layer=l,
                                       cols=(rw_cols, mla_cols, conv_cols), mod_base=3, seq=seq)

        y_rw = _rwkv(p_rw, rw_mu[l], rw_w0[l], rw_w2[l], rw_a0[l], rw_a2[l], rw_g2[l],
                     rw_k_k[l], rw_k_a[l], rw_r_k[l], rw_lnx_g[l], rw_lnx_b[l],
                     batch=batch, seq=seq)
        q, k, v = _mla_up(p_mla, tables, mla_q_norm_g[l], mla_w_uq[l], mla_kv_norm_g[l],
                          mla_w_ukv[l], seq=seq)
        y_mla = _attention(q, k, v, batch=batch, seq=seq)
        y_conv = _conv(p_conv, conv_w[l], conv_b[l], conv_ln_g[l], conv_ln_b[l],
                       batch=batch, seq=seq)

        xt = _out_proj(xt, mod, (y_rw, y_mla, y_conv), w_out, row(ln_g[l, 1]), row(ln_b[l, 1]),
                       layer=l, mod_base=3, alpha=alpha, seq=seq)

        xt = _ffn(xt, mod, w_ffn2_in, w_ffn2_out, row(ln_g[l, 2]), row(ln_b[l, 2]),
                  layer=l, mod_base=6, alpha=alpha, seq=seq)
    return xt.reshape(batch, seq, d)
```

```python
import functools

import jax
import jax.numpy as jnp
from jax import lax
from jax.experimental import pallas as pl
from jax.experimental.pallas import tpu as pltpu

F32 = jnp.float32
BF16 = jnp.bfloat16

LANES = 128
SUBLANES = 8
RW_HEAD_DIM = 64
RW_CHUNK = 64
RW_INV_BASE = 16
RW_OUT_DTYPE = jnp.bfloat16
RW_DECAY_LORA = 64
RW_AAA_LORA = 64
RW_GATE_LORA = 128
RW_GN_EPS = 64e-5
MLA_NOPE = 128
MLA_ROPE = 64
MLA_V = 128
MLA_Q_LORA = 384
MLA_KV_LORA = 256
MLA_QK_PAD = 256
MLA_SCALE = (MLA_NOPE + MLA_ROPE) ** -0.5
LOG2_E = 1.4426950408889634
ROPE_THETA = 10000.0
CONV_K = 31
CONV_HALO = 32
N_MOD = 9
FFN_RES = 0.5
LN_EPS = 1e-5
RMS_EPS = 1e-6
VMEM_LIMIT = 52 * 1024 * 1024
TOKEN_TILE = 512
ATTN_TILE = 512
RW_BLOCK = 512
FFN_TILE = 512
FFN_TOKEN_TILE = 512


def _cparams(*sem):
    return pltpu.CompilerParams(dimension_semantics=sem, vmem_limit_bytes=VMEM_LIMIT)


def _dot(a, b):
    return jnp.dot(a, b, preferred_element_type=F32)


def _dot_nt(a, b):
    return lax.dot_general(a, b, (((1,), (1,)), ((), ())), preferred_element_type=F32)


def _rw_dot(a, b):
    return _dot(a.astype(BF16), b.astype(BF16))


def _split_bf16(x, terms):
    parts = []
    for _ in range(terms):
        hi = x.astype(BF16)
        parts.append(hi)
        x = x - hi.astype(F32)
    return parts


def _dot_split_lhs(x, exact_rhs, terms):
    return sum(_dot(part, exact_rhs) for part in _split_bf16(x, terms))


def _dot_split_rhs(exact_lhs, x, terms):
    return sum(_dot(exact_lhs, part) for part in _split_bf16(x, terms))


def _mm3(a, b):
    a_hi, a_lo = _split_bf16(a, 2)
    b_hi, b_lo = _split_bf16(b, 2)
    k, n = b.shape
    if k != LANES:
        return _dot(a_hi, b_hi) + _dot(a_lo, b_hi) + _dot(a_hi, b_lo)
    lhs = jnp.concatenate([a_hi, a_lo], axis=1)
    if n != LANES:
        return _dot(lhs, jnp.concatenate([b_hi, b_hi], axis=0)) + _dot(a_hi, b_lo)
    rhs = jnp.concatenate([jnp.concatenate([b_hi, b_lo], axis=1),
                           jnp.concatenate([b_hi, jnp.zeros_like(b_lo)], axis=1)], axis=0)
    out = _dot(lhs, rhs)
    return out[:, :n] + out[:, n:]


def _sigmoid(x):
    return 1.0 / (1.0 + jnp.exp(-x))


def _silu(x):
    return x * _sigmoid(x)


def _layer_norm(y, g, b):
    mean = jnp.mean(y, -1, keepdims=True)
    d = y - mean
    var = jnp.mean(d * d, -1, keepdims=True)
    return d * lax.rsqrt(var + LN_EPS) * g + b


def _ada_kernel(ct_ref, w_ref, b_ref, o_ref, *, batch):
    w = w_ref[0]
    rows = [jnp.sum(_silu(ct_ref[:, b:b + 1]) * w, axis=0, keepdims=True) for b in range(batch)]
    o_ref[0] = jnp.concatenate(rows, axis=0) + b_ref[0]


def _ada(c, w_ada, b_ada):
    depth, d, n = w_ada.shape
    batch = c.shape[0]
    tn = 1024
    kern = functools.partial(_ada_kernel, batch=batch)
    return pl.pallas_call(
        kern,
        grid=(depth, n // tn),
        in_specs=[pl.BlockSpec((d, batch), lambda l, j: (0, 0)),
                  pl.BlockSpec((1, d, tn), lambda l, j: (l, 0, j)),
                  pl.BlockSpec((1, 1, tn), lambda l, j: (l, 0, j))],
        out_specs=pl.BlockSpec((1, batch, tn), lambda l, j: (l, 0, j)),
        out_shape=jax.ShapeDtypeStruct((depth, batch, n), F32),
        compiler_params=_cparams("parallel", "parallel"),
        name="ada",
    )(c.T, w_ada, b_ada.reshape(depth, 1, n))


def _ffn_kernel(x_ref, mod_ref, wg_ref, wu_ref, wo_ref, lng_ref, lnb_ref, o_ref,
                h_ref, *, mod_base, alpha):
    j = pl.program_id(1)
    last = pl.num_programs(1) - 1

    def contribution(h):
        gate = _dot(h, wg_ref[0])
        up = _dot(h, wu_ref[0])
        act = (_silu(gate) * up).astype(BF16)
        return _dot(act, wo_ref[0])

    @pl.when(j == 0)
    def _():
        sh = mod_ref[0, mod_base:mod_base + 1, :]
        sc = mod_ref[0, mod_base + 1:mod_base + 2, :]
        h = (x_ref[...] * (1.0 + sc) + sh).astype(BF16)
        h_ref[...] = h
        o_ref[...] = contribution(h)

    @pl.when((j > 0) & (j < last))
    def _():
        o_ref[...] += contribution(h_ref[...])

    @pl.when(j == last)
    def _():
        acc = o_ref[...] + contribution(h_ref[...])
        g = mod_ref[0, mod_base + 2:mod_base + 3, :]
        y = alpha * x_ref[...] + (FFN_RES * (1.0 + g)) * acc
        o_ref[...] = _layer_norm(y, lng_ref[...], lnb_ref[...])


def _ffn(x, mod, w_in, w_out, ln_g, ln_b, *, layer, mod_base, alpha, seq):
    t, d = x.shape
    f = w_out.shape[1]
    tm = min(FFN_TOKEN_TILE, seq)
    tf = FFN_TILE
    nf = f // tf
    assert nf >= 2
    per_b = seq // tm
    kern = functools.partial(_ffn_kernel, mod_base=mod_base, alpha=alpha)
    return pl.pallas_call(
        kern,
        grid=(t // tm, nf),
        in_specs=[pl.BlockSpec((tm, d), lambda i, j: (i, 0)),
                  pl.BlockSpec((1, N_MOD, d), lambda i, j: (i // per_b, 0, 0)),
                  pl.BlockSpec((1, d, tf), lambda i, j: (layer, 0, j)),
                  pl.BlockSpec((1, d, tf), lambda i, j: (layer, 0, nf + j)),
                  pl.BlockSpec((1, tf, d), lambda i, j: (layer, j, 0)),
                  pl.BlockSpec((1, d), lambda i, j: (0, 0)),
                  pl.BlockSpec((1, d), lambda i, j: (0, 0))],
        out_specs=pl.BlockSpec((tm, d), lambda i, j: (i, 0)),
        out_shape=jax.ShapeDtypeStruct((t, d), F32),
        scratch_shapes=[pltpu.VMEM((tm, d), BF16)],
        compiler_params=_cparams("parallel", "arbitrary"),
        name="ffn",
    )(x, mod, w_in, w_in, w_out, ln_g, ln_b)


def _in_proj_kernel(x_ref, mod_ref, w_rw_ref, w_mla_ref, w_conv_ref,
                    rw_ref, mla_ref, conv_ref, *, mod_base):
    sh = mod_ref[0, mod_base:mod_base + 1, :]
    sc = mod_ref[0, mod_base + 1:mod_base + 2, :]
    h = (x_ref[...] * (1.0 + sc) + sh).astype(BF16)
    rw_ref[...] = _dot(h, w_rw_ref[0])
    mla_ref[...] = _dot(h, w_mla_ref[0])
    conv_ref[...] = _dot(h, w_conv_ref[0])


def _in_proj(x, mod, w_in, *, layer, cols, mod_base, seq):
    t, d = x.shape
    rw_cols, mla_cols, conv_cols = cols
    tm = min(TOKEN_TILE, seq)
    per_b = seq // tm
    w_mla = w_in[layer:layer + 1, :, rw_cols:rw_cols + mla_cols]
    w_conv = w_in[layer:layer + 1, :, rw_cols + mla_cols:]
    tok = lambda n: pl.BlockSpec((tm, n), lambda i: (i, 0))
    resident = lambda n, l: pl.BlockSpec((1, d, n), lambda i: (l, 0, 0),
                                         pipeline_mode=pl.Buffered(1))
    kern = functools.partial(_in_proj_kernel, mod_base=mod_base)
    return pl.pallas_call(
        kern,
        grid=(t // tm,),
        in_specs=[tok(d), pl.BlockSpec((1, N_MOD, d), lambda i: (i // per_b, 0, 0)),
                  resident(rw_cols, layer), resident(mla_cols, 0), resident(conv_cols, 0)],
        out_specs=[tok(rw_cols), tok(mla_cols), tok(conv_cols)],
        out_shape=[jax.ShapeDtypeStruct((t, n), F32) for n in cols],
        compiler_params=_cparams("parallel"),
        name="in_proj",
    )(x, mod, w_in, w_mla, w_conv)


def _rwkv_kernel(r_ref, k_ref, v_ref, lo_ref, mur_ref, muk_ref, muv_ref, mulo_ref,
                 w0_ref, w2_ref, a0_ref, a2_ref, g2_ref, kk_ref, ka_ref, rk_ref,
                 lng_ref, lnb_ref, o_ref,
                 state_ref, pr_ref, pk_ref, pv_ref, plo_ref, obuf_ref, *, bb, tb):
    j = pl.program_id(1)
    L = RW_CHUNK
    nc = tb // L

    @pl.when(j == 0)
    def _():
        state_ref[...] = jnp.zeros_like(state_ref)
        pr_ref[...] = jnp.zeros_like(pr_ref)
        pk_ref[...] = jnp.zeros_like(pk_ref)
        pv_ref[...] = jnp.zeros_like(pv_ref)
        plo_ref[...] = jnp.zeros_like(plo_ref)

    lane_r = lax.broadcasted_iota(jnp.int32, (LANES, LANES), 0)
    lane_c = lax.broadcasted_iota(jnp.int32, (LANES, LANES), 1)

    def same_block(size):
        return (lane_r // size) == (lane_c // size)

    same_head = same_block(RW_HEAD_DIM).astype(BF16)
    stack_mask = ((lane_r // L) == (lane_c // RW_HEAD_DIM)).astype(F32)
    strict_lower = lane_r > lane_c
    lower = lane_r >= lane_c
    eye = (lane_r == lane_c).astype(F32)
    tri = (lax.broadcasted_iota(jnp.int32, (L, L), 0)
           >= lax.broadcasted_iota(jnp.int32, (L, L), 1)).astype(BF16)
    zeros = jnp.zeros((LANES, LANES), F32)

    def head_sum(x):
        return _dot_split_lhs(x, same_head, 2)

    def stack(x):
        return jnp.concatenate([x, x], axis=0) * stack_mask

    def prologue(b):
        def shift_mix(p_ref, prev_ref, mu_ref):
            p = p_ref[b]
            row = lax.broadcasted_iota(jnp.int32, p.shape, 0)
            prev = jnp.where(row == 0, prev_ref[b, 0:1, :], pltpu.roll(p, 1, axis=0))
            prev_ref[b, 0:1, :] = p[tb - 1:tb, :]
            return p + (prev - p) * mu_ref[...]

        r = shift_mix(r_ref, pr_ref, mur_ref)
        k = shift_mix(k_ref, pk_ref, muk_ref)
        v = shift_mix(v_ref, pv_ref, muv_ref)
        lo = shift_mix(lo_ref, plo_ref, mulo_ref)
        w_lo = lo[:, :RW_DECAY_LORA]
        a_lo = lo[:, RW_DECAY_LORA:RW_DECAY_LORA + RW_AAA_LORA]
        g_lo = lo[:, RW_DECAY_LORA + RW_AAA_LORA:]
        z = w0_ref[...] + _rw_dot(jnp.tanh(w_lo), w2_ref[...])
        softplus_neg_z = jnp.maximum(-z, 0.0) + jnp.log(1.0 + jnp.exp(-jnp.abs(z)))
        log_decay = -jnp.exp(-softplus_neg_z - 0.5)
        a = _sigmoid(a0_ref[...] + _rw_dot(a_lo, a2_ref[...]))
        gate = _rw_dot(_sigmoid(g_lo), g2_ref[...])
        kk = k * kk_ref[...]
        kk = kk / jnp.maximum(jnp.sqrt(head_sum(kk * kk)), 1e-12)
        k = k * (1.0 + (a - 1.0) * ka_ref[...])
        return r, k, v, kk, a, gate, log_decay

    seqs = [prologue(b) for b in range(bb)]
    items = [(b, c) for c in range(nc) for b in range(bb)]
    every = range(len(items))

    def chunk_operands(b, c):
        r, k, v, kk, a, _, log_decay = seqs[b]
        rows = slice(c * L, (c + 1) * L)
        lw = log_decay[rows]
        cum = _dot_split_rhs(tri, lw, 3)
        w_cum = jnp.exp(cum)
        w_inv = jnp.exp(-cum)
        w_prev = jnp.exp(cum - lw)
        w_last = w_cum[L - 1:L, :]
        kk_c = kk[rows]
        return (stack(-kk_c * w_prev), stack(kk_c * a[rows] * w_inv), stack(k[rows] * w_inv),
                stack(r[rows] * w_cum), stack(v[rows]), w_last)

    ops = [chunk_operands(b, c) for b, c in items]
    a2_ = [o[0] for o in ops]
    b2_ = [o[1] for o in ops]
    k2_ = [o[2] for o in ops]
    r2_ = [o[3] for o in ops]
    v2_ = [o[4] for o in ops]
    w_last = [o[5] for o in ops]

    scores = [_dot_nt(jnp.concatenate([a2_[i], r2_[i]], axis=0).astype(BF16),
                      jnp.concatenate([b2_[i], k2_[i]], axis=0).astype(BF16)) for i in every]
    a_ab = [jnp.where(strict_lower, s[:LANES, :LANES], 0.0) for s in scores]
    a_ak = [jnp.where(strict_lower, s[:LANES, LANES:], 0.0) for s in scores]
    a_rb = [jnp.where(lower, s[LANES:, :LANES], 0.0) for s in scores]
    a_rk = [jnp.where(lower, s[LANES:, LANES:], 0.0) for s in scores]

    diag_blocks = same_block(RW_INV_BASE)
    p = [jnp.where(diag_blocks, x, 0.0) for x in a_ab]
    t = [eye + x for x in p]
    n = 2
    while n < RW_INV_BASE:
        p = [_mm3(x, x) for x in p]
        t = [t[i] + _mm3(t[i], p[i]) for i in every]
        n *= 2
    size = 2 * RW_INV_BASE
    while size <= L:
        off_diag = same_block(size) & ~same_block(size // 2)
        et = [_mm3(jnp.where(off_diag, a_ab[i], 0.0), t[i]) for i in every]
        t = [t[i] + _mm3(t[i], et[i]) for i in every]
        size *= 2

    akv = [_rw_dot(a_ak[i], v2_[i]) for i in every]
    x = [_mm3(t[i], jnp.concatenate([a2_[i], akv[i]], axis=1)) for i in every]
    big = []
    for i in every:
        lhs = jnp.concatenate(
            [jnp.concatenate([a_rb[i], a_rk[i]], axis=1),
             jnp.concatenate([(b2_[i] * w_last[i]).T, (k2_[i] * w_last[i]).T], axis=1)], axis=0)
        rhs = jnp.concatenate([x[i], jnp.concatenate([zeros, v2_[i]], axis=1)], axis=0)
        big.append(_mm3(lhs, rhs))

    states = [state_ref[b] for b in range(bb)]
    for i, (b, c) in enumerate(items):
        r_hat = r2_[i] + big[i][:LANES, :LANES]
        st = _rw_dot(jnp.concatenate([r_hat, big[i][LANES:, :LANES]], axis=0), states[b])
        o2 = st[:LANES] + big[i][:LANES, LANES:]
        obuf_ref[b, c * L:(c + 1) * L, :] = o2[:L] + o2[L:]
        w_rows = jnp.broadcast_to(w_last[i], (LANES, LANES)).T
        states[b] = states[b] * w_rows + st[LANES:] + big[i][LANES:, LANES:]
    for b in range(bb):
        state_ref[b] = states[b]

    inv_n = 1.0 / RW_HEAD_DIM
    for b in range(bb):
        r, k, v, _, _, gate, _ = seqs[b]
        o = obuf_ref[b]
        mean = head_sum(o) * inv_n
        d = o - mean
        var = head_sum(d * d) * inv_n
        on = d * lax.rsqrt(var + RW_GN_EPS) * lng_ref[...] + lnb_ref[...]
        bonus = head_sum(r * k * rk_ref[...]) * v
        o_ref[b] = ((on + bonus) * gate).astype(o_ref.dtype)


def _rwkv(p_rw, mu, w0, w2, a0, a2, g2, k_k, k_a, r_k, lnx_g, lnx_b, *, batch, seq):
    width = w0.shape[-1]
    pairs = width // LANES
    tb = min(RW_BLOCK, seq)
    lo_w = RW_DECAY_LORA + RW_AAA_LORA + RW_GATE_LORA
    lo_blk = 3 * width // lo_w
    p3 = p_rw.reshape(batch, seq, p_rw.shape[-1])

    def col(off):
        return lambda h, j: (0, j, off + h)

    def vec(off):
        return lambda h, j: (0, off + h)

    row = lambda a: a.reshape(1, -1)
    kern = functools.partial(_rwkv_kernel, bb=batch, tb=tb)
    out = pl.pallas_call(
        kern,
        grid=(pairs, seq // tb),
        in_specs=[pl.BlockSpec((batch, tb, LANES), col(0)),
                  pl.BlockSpec((batch, tb, LANES), col(pairs)),
                  pl.BlockSpec((batch, tb, LANES), col(2 * pairs)),
                  pl.BlockSpec((batch, tb, lo_w), lambda h, j: (0, j, lo_blk)),
                  pl.BlockSpec((1, LANES), vec(0)),
                  pl.BlockSpec((1, LANES), vec(pairs)),
                  pl.BlockSpec((1, LANES), vec(2 * pairs)),
                  pl.BlockSpec((1, lo_w), lambda h, j: (0, lo_blk)),
                  pl.BlockSpec((1, LANES), vec(0)),
                  pl.BlockSpec((RW_DECAY_LORA, LANES), vec(0)),
                  pl.BlockSpec((1, LANES), vec(0)),
                  pl.BlockSpec((RW_AAA_LORA, LANES), vec(0)),
                  pl.BlockSpec((RW_GATE_LORA, LANES), vec(0)),
                  pl.BlockSpec((1, LANES), vec(0)),
                  pl.BlockSpec((1, LANES), vec(0)),
                  pl.BlockSpec((1, LANES), vec(0)),
                  pl.BlockSpec((1, LANES), vec(0)),
                  pl.BlockSpec((1, LANES), vec(0))],
        out_specs=pl.BlockSpec((batch, tb, LANES), lambda h, j: (0, j, h)),
        out_shape=jax.ShapeDtypeStruct((batch, seq, width), RW_OUT_DTYPE),
        scratch_shapes=[pltpu.VMEM((batch, LANES, LANES), F32),
                        pltpu.VMEM((batch, SUBLANES, LANES), F32),
                        pltpu.VMEM((batch, SUBLANES, LANES), F32),
                        pltpu.VMEM((batch, SUBLANES, LANES), F32),
                        pltpu.VMEM((batch, SUBLANES, lo_w), F32),
                        pltpu.VMEM((batch, tb, LANES), F32)],
        compiler_params=_cparams("parallel", "arbitrary"),
        name="rwkv",
    )(p3, p3, p3, p3, row(mu), row(mu), row(mu), row(mu), row(w0), w2, row(a0), a2, g2,
      row(k_k), row(k_a), row(r_k), row(lnx_g), row(lnx_b))
    return out.reshape(batch * seq, width)


def _rope_table_kernel(pos_ref, freq_ref, cos_ref, sa_ref, sb_ref):
    half = MLA_ROPE // 2
    ang = pos_ref[...].astype(F32) * freq_ref[...]
    lane = lax.broadcasted_iota(jnp.int32, ang.shape, 1)
    cos = jnp.cos(ang)
    sin = jnp.sin(ang)
    cos_ref[...] = jnp.where(lane < MLA_ROPE, cos, 1.0)
    sa_ref[...] = jnp.where(lane < half, -sin, 0.0)
    sb_ref[...] = jnp.where((lane >= half) & (lane < MLA_ROPE), sin, 0.0)


def _rope_tables(positions):
    t = positions.size
    half = MLA_ROPE // 2
    tm = min(2048, t)
    inv_freq = ROPE_THETA ** (-jnp.arange(half, dtype=F32) / half)
    freq = jnp.concatenate([inv_freq, inv_freq, jnp.zeros((LANES - MLA_ROPE,), F32)])
    spec = pl.BlockSpec((tm, LANES), lambda i: (i, 0))
    shp = jax.ShapeDtypeStruct((t, LANES), F32)
    return pl.pallas_call(
        _rope_table_kernel,
        grid=(t // tm,),
        in_specs=[pl.BlockSpec((tm, 1), lambda i: (i, 0)),
                  pl.BlockSpec((1, LANES), lambda i: (0, 0))],
        out_specs=[spec, spec, spec],
        out_shape=[shp, shp, shp],
        compiler_params=_cparams("parallel"),
        name="rope_tables",
    )(positions.reshape(t, 1), freq.reshape(1, LANES))


def _rope(x, cos, sa, sb):
    half = MLA_ROPE // 2
    return x * cos + pltpu.roll(x, LANES - half, axis=1) * sa + pltpu.roll(x, half, axis=1) * sb


def _mla_up_kernel(p_ref, cos_ref, sa_ref, sb_ref, qg_ref, wq_ref, kvg_ref, wk_ref, wvt_ref,
                   q_ref, k_ref, vt_ref, *, heads):
    p = p_ref[...]
    cos, sa, sb = cos_ref[...], sa_ref[...], sb_ref[...]

    def rms(x, g):
        return x * lax.rsqrt(jnp.mean(x * x, -1, keepdims=True) + RMS_EPS) * g

    q_lat = rms(p[:, :MLA_Q_LORA], qg_ref[...]).astype(BF16)
    kv_lat = rms(p[:, MLA_Q_LORA:MLA_Q_LORA + MLA_KV_LORA], kvg_ref[...]).astype(BF16)
    k_pe = p[:, MLA_Q_LORA + MLA_KV_LORA:]
    k_pe = jnp.concatenate([k_pe, jnp.zeros((k_pe.shape[0], LANES - MLA_ROPE), F32)], axis=1)
    k_pe = _rope(k_pe, cos, sa, sb).astype(BF16)

    q = _dot(q_lat, wq_ref[...]) * (MLA_SCALE * LOG2_E)
    k_nope = _dot(kv_lat, wk_ref[...])
    vt_ref[0] = _dot_nt(wvt_ref[...], kv_lat).astype(BF16)
    for h in range(heads):
        base = h * MLA_QK_PAD
        q_ref[:, base:base + MLA_NOPE] = q[:, base:base + MLA_NOPE].astype(BF16)
        q_ref[:, base + MLA_NOPE:base + MLA_QK_PAD] = _rope(
            q[:, base + MLA_NOPE:base + MLA_QK_PAD], cos, sa, sb).astype(BF16)
        k_ref[:, base:base + MLA_NOPE] = k_nope[:, h * MLA_NOPE:(h + 1) * MLA_NOPE].astype(BF16)
        k_ref[:, base + MLA_NOPE:base + MLA_QK_PAD] = k_pe


def _mla_up(p_mla, tables, q_norm_g, w_uq, kv_norm_g, w_ukv, *, seq):
    t, cols = p_mla.shape
    heads = w_uq.shape[1] // (MLA_NOPE + MLA_ROPE)
    tm = min(TOKEN_TILE, seq)
    wq = w_uq.reshape(MLA_Q_LORA, heads, MLA_NOPE + MLA_ROPE)
    wq = jnp.pad(wq, ((0, 0), (0, 0), (0, MLA_QK_PAD - MLA_NOPE - MLA_ROPE)))
    wq = wq.reshape(MLA_Q_LORA, heads * MLA_QK_PAD).astype(BF16)
    wkv = w_ukv.reshape(MLA_KV_LORA, heads, MLA_NOPE + MLA_V)
    wk = wkv[:, :, :MLA_NOPE].reshape(MLA_KV_LORA, heads * MLA_NOPE).astype(BF16)
    wvt = wkv[:, :, MLA_NOPE:].reshape(MLA_KV_LORA, heads * MLA_V).T.astype(BF16)
    tok = lambda n: pl.BlockSpec((tm, n), lambda i: (i, 0))
    full = lambda a: pl.BlockSpec(a.shape, lambda i: (0, 0))
    qg = q_norm_g.reshape(1, -1)
    kvg = kv_norm_g.reshape(1, -1)
    kern = functools.partial(_mla_up_kernel, heads=heads)
    return pl.pallas_call(
        kern,
        grid=(t // tm,),
        in_specs=[tok(cols), tok(LANES), tok(LANES), tok(LANES),
                  full(qg), full(wq), full(kvg), full(wk), full(wvt)],
        out_specs=[tok(heads * MLA_QK_PAD), tok(heads * MLA_QK_PAD),
                   pl.BlockSpec((1, heads * MLA_V, tm), lambda i: (i, 0, 0))],
        out_shape=[jax.ShapeDtypeStruct((t, heads * MLA_QK_PAD), BF16),
                   jax.ShapeDtypeStruct((t, heads * MLA_QK_PAD), BF16),
                   jax.ShapeDtypeStruct((t // tm, heads * MLA_V, tm), BF16)],
        compiler_params=_cparams("parallel"),
        name="mla_up",
    )(p_mla, *tables, qg, wq, kvg, wk, wvt)


def _attn_kernel(qi_ref, kj_ref, q_ref, k_ref, vt_ref, o_ref, m_ref, l_ref, acc_ref, *, heads):
    step_id = pl.program_id(1)
    i = qi_ref[step_id]
    j = kj_ref[step_id]

    @pl.when(j == 0)
    def _():
        m_ref[...] = jnp.full_like(m_ref, -jnp.inf)
        l_ref[...] = jnp.zeros_like(l_ref)
        acc_ref[...] = jnp.zeros_like(acc_ref)

    def qk(h):
        return slice(h * MLA_QK_PAD, (h + 1) * MLA_QK_PAD)

    def vo(h):
        return slice(h * MLA_V, (h + 1) * MLA_V)

    def step(masked):
        hs = range(heads)
        s = [_dot_nt(k_ref[:, qk(h)], q_ref[:, qk(h)]) for h in hs]
        if masked:
            key = lax.broadcasted_iota(jnp.int32, s[0].shape, 0)
            query = lax.broadcasted_iota(jnp.int32, s[0].shape, 1)
            s = [jnp.where(key <= query, x, -jnp.inf) for x in s]
        m_prev = [m_ref[h] for h in hs]
        m_new = [jnp.maximum(m_prev[h], jnp.max(s[h], 0, keepdims=True)) for h in hs]
        alpha = [jnp.exp2(m_prev[h] - m_new[h]) for h in hs]
        p = [jnp.exp2(s[h] - m_new[h]) for h in hs]
        pv = [_dot(vt_ref[0, vo(h), :], p[h].astype(BF16)) for h in hs]
        for h in hs:
            l_ref[h] = alpha[h] * l_ref[h] + jnp.sum(p[h], 0, keepdims=True)
            acc_ref[vo(h), :] = alpha[h] * acc_ref[vo(h), :] + pv[h]
            m_ref[h] = m_new[h]

    @pl.when(j < i)
    def _():
        step(False)

    @pl.when(j == i)
    def _():
        step(True)
        for h in range(heads):
            o_ref[:, vo(h)] = (acc_ref[vo(h), :] / l_ref[h]).T.astype(o_ref.dtype)


def _attention(q, k, vt, *, batch, seq):
    t = q.shape[0]
    heads = vt.shape[1] // MLA_V
    tq = min(ATTN_TILE, seq)
    assert vt.shape[2] == tq
    nq = seq // tq
    pairs = [(i, j) for i in range(nq) for j in range(i + 1)]
    qi = jnp.asarray([p[0] for p in pairs], jnp.int32)
    kj = jnp.asarray([p[1] for p in pairs], jnp.int32)
    kern = functools.partial(_attn_kernel, heads=heads)
    grid_spec = pltpu.PrefetchScalarGridSpec(
        num_scalar_prefetch=2,
        grid=(batch, len(pairs)),
        in_specs=[pl.BlockSpec((tq, heads * MLA_QK_PAD), lambda b, s, qi, kj: (b * nq + qi[s], 0)),
                  pl.BlockSpec((tq, heads * MLA_QK_PAD), lambda b, s, qi, kj: (b * nq + kj[s], 0)),
                  pl.BlockSpec((1, heads * MLA_V, tq),
                               lambda b, s, qi, kj: (b * nq + kj[s], 0, 0))],
        out_specs=pl.BlockSpec((tq, heads * MLA_V), lambda b, s, qi, kj: (b * nq + qi[s], 0)),
        scratch_shapes=[pltpu.VMEM((heads, 1, tq), F32), pltpu.VMEM((heads, 1, tq), F32),
                        pltpu.VMEM((heads * MLA_V, tq), F32)])
    return pl.pallas_call(
        kern,
        grid_spec=grid_spec,
        out_shape=jax.ShapeDtypeStruct((t, heads * MLA_V), BF16),
        compiler_params=_cparams("parallel", "arbitrary"),
        name="attention",
    )(qi, kj, q, k, vt)


def _conv_kernel(lin_ref, gate_ref, w_ref, b_ref, lng_ref, lnb_ref, o_ref, u_ref, *, tm):
    j = pl.program_id(1)

    @pl.when(j == 0)
    def _():
        u_ref[0:CONV_HALO, :] = jnp.zeros((CONV_HALO, u_ref.shape[1]), F32)

    u_ref[CONV_HALO:, :] = lin_ref[0] * _sigmoid(gate_ref[0])
    first = CONV_HALO - (CONV_K - 1)
    u = u_ref[...]
    rows = u.shape[0]
    acc = jnp.zeros((tm, u.shape[1]), F32)
    for phase in range(SUBLANES):
        shifted = u if phase == 0 else pltpu.roll(u, rows - phase, axis=0)
        for base in range(0, CONV_HALO + 1, SUBLANES):
            tap = base + phase - first
            if 0 <= tap < CONV_K:
                acc = acc + shifted[base:base + tm, :] * w_ref[tap:tap + 1, :]
    halo = u_ref[tm:tm + CONV_HALO, :]
    u_ref[0:CONV_HALO, :] = halo
    y = _layer_norm(acc + b_ref[...], lng_ref[...], lnb_ref[...])
    o_ref[0] = _silu(y).astype(o_ref.dtype)


def _conv(p_conv, conv_w, conv_b, ln_g, ln_b, *, batch, seq):
    ch = conv_w.shape[1]
    tm = min(TOKEN_TILE, seq)
    p3 = p_conv.reshape(batch, seq, 2 * ch)
    full = lambda a: pl.BlockSpec(a.shape, lambda b, j: (0, 0))
    row = lambda a: a.reshape(1, -1)
    kern = functools.partial(_conv_kernel, tm=tm)
    out = pl.pallas_call(
        kern,
        grid=(batch, seq // tm),
        in_specs=[pl.BlockSpec((1, tm, ch), lambda b, j: (b, j, 0)),
                  pl.BlockSpec((1, tm, ch), lambda b, j: (b, j, 1)),
                  full(conv_w), full(row(conv_b)), full(row(ln_g)), full(row(ln_b))],
        out_specs=pl.BlockSpec((1, tm, ch), lambda b, j: (b, j, 0)),
        out_shape=jax.ShapeDtypeStruct((batch, seq, ch), BF16),
        scratch_shapes=[pltpu.VMEM((CONV_HALO + tm, ch), F32)],
        compiler_params=_cparams("parallel", "arbitrary"),
        name="conv",
    )(p3, p3, conv_w, row(conv_b), row(ln_g), row(ln_b))
    return out.reshape(batch * seq, ch)


def _out_proj_kernel(x_ref, mod_ref, y1_ref, y2_ref, y3_ref, w1_ref, w2_ref, w3_ref,
                     lng_ref, lnb_ref, o_ref, *, mod_base, alpha):
    y = (_dot(y1_ref[...], w1_ref[0]) + _dot(y2_ref[...], w2_ref[0])
         + _dot(y3_ref[...], w3_ref[0]))
    g = mod_ref[0, mod_base + 2:mod_base + 3, :]
    o_ref[...] = _layer_norm(alpha * x_ref[...] + (1.0 + g) * y, lng_ref[...], lnb_ref[...])


def _out_proj(x, mod, ys, w_out, ln_g, ln_b, *, layer, mod_base, alpha, seq):
    t, d = x.shape
    tm = min(TOKEN_TILE, seq)
    per_b = seq // tm
    tok = lambda n: pl.BlockSpec((tm, n), lambda i: (i, 0))
    full = lambda a: pl.BlockSpec(a.shape, lambda i: (0, 0))
    w_specs = []
    offset = 0
    for y in ys:
        width = y.shape[1]
        assert offset % width == 0
        w_specs.append(pl.BlockSpec((1, width, d), functools.partial(
            lambda i, blk: (layer, blk, 0), blk=offset // width)))
        offset += width
    assert offset == w_out.shape[1]
    kern = functools.partial(_out_proj_kernel, mod_base=mod_base, alpha=alpha)
    return pl.pallas_call(
        kern,
        grid=(t // tm,),
        in_specs=[tok(d), pl.BlockSpec((1, N_MOD, d), lambda i: (i // per_b, 0, 0)),
                  *[tok(y.shape[1]) for y in ys], *w_specs, full(ln_g), full(ln_b)],
        out_specs=tok(d),
        out_shape=jax.ShapeDtypeStruct((t, d), F32),
        compiler_params=_cparams("parallel"),
        name="out_proj",
    )(x, mod, *ys, w_out, w_out, w_out, ln_g, ln_b)


def kernel(x, c, positions, w_ada, b_ada, ln_g, ln_b, w_ffn1_in, w_ffn1_out, w_ffn2_in, w_ffn2_out, w_in, w_out, rw_mu, rw_w0, rw_w2, rw_a0, rw_a2, rw_g2, rw_k_k, rw_k_a, rw_r_k, rw_lnx_g, rw_lnx_b, mla_q_norm_g, mla_w_uq, mla_kv_norm_g, mla_w_ukv, conv_w, conv_b, conv_ln_g, conv_ln_b):
    batch, seq, d = x.shape
    depth = w_ada.shape[0]
    alpha = (2 * depth) ** 0.25
    rw_cols = rw_mu.shape[1]
    mla_cols = MLA_Q_LORA + MLA_KV_LORA + MLA_ROPE
    conv_cols = w_in.shape[2] - rw_cols - mla_cols

    mod_all = _ada(c, w_ada, b_ada).reshape(depth, batch, N_MOD, d)
    tables = _rope_tables(positions)
    xt = x.reshape(batch * seq, d)
    row = lambda a: a.reshape(1, -1)
    w_ffn1_in, w_ffn1_out, w_ffn2_in, w_ffn2_out, w_in, w_out = (
        w.astype(BF16) for w in (w_ffn1_in, w_ffn1_out, w_ffn2_in, w_ffn2_out, w_in, w_out))

    for l in range(depth):
        mod = mod_all[l]
        xt = _ffn(xt, mod, w_ffn1_in, w_ffn1_out, row(ln_g[l, 0]), row(ln_b[l, 0]),
                  layer=l, mod_base=0, alpha=alpha, seq=seq)

        p_rw, p_mla, p_conv = _in_proj(xt, mod, w_in, layer=l,
                                       cols=(rw_cols, mla_cols, conv_cols), mod_base=3, seq=seq)

        y_rw = _rwkv(p_rw, rw_mu[l], rw_w0[l], rw_w2[l], rw_a0[l], rw_a2[l], rw_g2[l],
                     rw_k_k[l], rw_k_a[l], rw_r_k[l], rw_lnx_g[l], rw_lnx_b[l],
                     batch=batch, seq=seq)
        q, k, v = _mla_up(p_mla, tables, mla_q_norm_g[l], mla_w_uq[l], mla_kv_norm_g[l],
                          mla_w_ukv[l], seq=seq)
        y_mla = _attention(q, k, v, batch=batch, seq=seq)
        y_conv = _conv(p_conv, conv_w[l], conv_b[l], conv_ln_g[l], conv_ln_b[l],
                       batch=batch, seq=seq)

        xt = _out_proj(xt, mod, (y_rw, y_mla, y_conv), w_out, row(ln_g[l, 1]), row(ln_b[l, 1]),
                       layer=l, mod_base=3, alpha=alpha, seq=seq)

        xt = _ffn(xt, mod, w_ffn2_in, w_ffn2_out, row(ln_g[l, 2]), row(ln_b[l, 2]),
                  layer=l, mod_base=6, alpha=alpha, seq=seq)
    return xt.reshape(batch, seq, d)
```

```python
import functools

import jax
import jax.numpy as jnp
from jax import lax
from jax.experimental import pallas as pl
from jax.experimental.pallas import tpu as pltpu

F32 = jnp.float32
BF16 = jnp.bfloat16

LANES = 128
SUBLANES = 8
RW_HEAD_DIM = 64
RW_CHUNK = 64
RW_INV_BASE = 16
RW_OUT_DTYPE = jnp.bfloat16
RW_DECAY_LORA = 64
RW_AAA_LORA = 64
RW_GATE_LORA = 128
RW_GN_EPS = 64e-5
MLA_NOPE = 128
MLA_ROPE = 64
MLA_V = 128
MLA_Q_LORA = 384
MLA_KV_LORA = 256
MLA_QK_PAD = 256
MLA_SCALE = (MLA_NOPE + MLA_ROPE) ** -0.5
LOG2_E = 1.4426950408889634
ROPE_THETA = 10000.0
CONV_K = 31
CONV_HALO = 32
N_MOD = 9
FFN_RES = 0.5
LN_EPS = 1e-5
RMS_EPS = 1e-6
VMEM_LIMIT = 52 * 1024 * 1024
TOKEN_TILE = 512
ATTN_TILE = 512
RW_BLOCK = 512
FFN_TILE = 512
FFN_TOKEN_TILE = 512


def _cparams(*sem):
    return pltpu.CompilerParams(dimension_semantics=sem, vmem_limit_bytes=VMEM_LIMIT)


def _dot(a, b):
    return jnp.dot(a, b, preferred_element_type=F32)


def _dot_nt(a, b):
    return lax.dot_general(a, b, (((1,), (1,)), ((), ())), preferred_element_type=F32)


def _rw_dot(a, b):
    return _dot(a.astype(BF16), b.astype(BF16))


def _split_bf16(x, terms):
    parts = []
    for _ in range(terms):
        hi = x.astype(BF16)
        parts.append(hi)
        x = x - hi.astype(F32)
    return parts


def _dot_split_lhs(x, exact_rhs, terms):
    return sum(_dot(part, exact_rhs) for part in _split_bf16(x, terms))


def _dot_split_rhs(exact_lhs, x, terms):
    return sum(_dot(exact_lhs, part) for part in _split_bf16(x, terms))


def _mm3(a, b):
    a_hi, a_lo = _split_bf16(a, 2)
    b_hi, b_lo = _split_bf16(b, 2)
    k, n = b.shape
    if k != LANES:
        return _dot(a_hi, b_hi) + _dot(a_lo, b_hi) + _dot(a_hi, b_lo)
    lhs = jnp.concatenate([a_hi, a_lo], axis=1)
    if n != LANES:
        return _dot(lhs, jnp.concatenate([b_hi, b_hi], axis=0)) + _dot(a_hi, b_lo)
    rhs = jnp.concatenate([jnp.concatenate([b_hi, b_lo], axis=1),
                           jnp.concatenate([b_hi, jnp.zeros_like(b_lo)], axis=1)], axis=0)
    out = _dot(lhs, rhs)
    return out[:, :n] + out[:, n:]


def _sigmoid(x):
    return 1.0 / (1.0 + jnp.exp(-x))


def _silu(x):
    return x * _sigmoid(x)


def _layer_norm(y, g, b):
    mean = jnp.mean(y, -1, keepdims=True)
    d = y - mean
    var = jnp.mean(d * d, -1, keepdims=True)
    return d * lax.rsqrt(var + LN_EPS) * g + b


def _ada_kernel(ct_ref, w_ref, b_ref, o_ref, *, batch):
    w = w_ref[0]
    rows = [jnp.sum(_silu(ct_ref[:, b:b + 1]) * w, axis=0, keepdims=True) for b in range(batch)]
    o_ref[0] = jnp.concatenate(rows, axis=0) + b_ref[0]


def _ada(c, w_ada, b_ada):
    depth, d, n = w_ada.shape
    batch = c.shape[0]
    tn = 1024
    kern = functools.partial(_ada_kernel, batch=batch)
    return pl.pallas_call(
        kern,
        grid=(depth, n // tn),
        in_specs=[pl.BlockSpec((d, batch), lambda l, j: (0, 0)),
                  pl.BlockSpec((1, d, tn), lambda l, j: (l, 0, j)),
                  pl.BlockSpec((1, 1, tn), lambda l, j: (l, 0, j))],
        out_specs=pl.BlockSpec((1, batch, tn), lambda l, j: (l, 0, j)),
        out_shape=jax.ShapeDtypeStruct((depth, batch, n), F32),
        compiler_params=_cparams("parallel", "parallel"),
        name="ada",
    )(c.T, w_ada, b_ada.reshape(depth, 1, n))


def _ffn_kernel(x_ref, mod_ref, wg_ref, wu_ref, wo_ref, lng_ref, lnb_ref, o_ref,
                h_ref, *, mod_base, alpha):
    j = pl.program_id(1)
    last = pl.num_programs(1) - 1

    def contribution(h):
        gate = _dot(h, wg_ref[0, 0])
        up = _dot(h, wu_ref[0, 0])
        act = (_silu(gate) * up).astype(BF16)
        return _dot(act, wo_ref[0])

    @pl.when(j == 0)
    def _():
        sh = mod_ref[0, mod_base:mod_base + 1, :]
        sc = mod_ref[0, mod_base + 1:mod_base + 2, :]
        h = (x_ref[...] * (1.0 + sc) + sh).astype(BF16)
        h_ref[...] = h
        o_ref[...] = contribution(h)

    @pl.when((j > 0) & (j < last))
    def _():
        o_ref[...] += contribution(h_ref[...])

    @pl.when(j == last)
    def _():
        acc = o_ref[...] + contribution(h_ref[...])
        g = mod_ref[0, mod_base + 2:mod_base + 3, :]
        y = alpha * x_ref[...] + (FFN_RES * (1.0 + g)) * acc
        o_ref[...] = _layer_norm(y, lng_ref[...], lnb_ref[...])


def _ffn(x, mod, w_in, w_out, ln_g, ln_b, *, layer, mod_base, alpha, seq):
    t, d = x.shape
    f = w_out.shape[1]
    tm = min(FFN_TOKEN_TILE, seq)
    tf = FFN_TILE
    nf = f // tf
    assert w_in.shape[1:] == (2 * nf, d, tf)
    assert nf >= 2
    per_b = seq // tm
    kern = functools.partial(_ffn_kernel, mod_base=mod_base, alpha=alpha)
    return pl.pallas_call(
        kern,
        grid=(t // tm, nf),
        in_specs=[pl.BlockSpec((tm, d), lambda i, j: (i, 0)),
                  pl.BlockSpec((1, N_MOD, d), lambda i, j: (i // per_b, 0, 0)),
                  pl.BlockSpec((1, 1, d, tf), lambda i, j: (layer, j, 0, 0)),
                  pl.BlockSpec((1, 1, d, tf), lambda i, j: (layer, nf + j, 0, 0)),
                  pl.BlockSpec((1, tf, d), lambda i, j: (layer, j, 0)),
                  pl.BlockSpec((1, d), lambda i, j: (0, 0)),
                  pl.BlockSpec((1, d), lambda i, j: (0, 0))],
        out_specs=pl.BlockSpec((tm, d), lambda i, j: (i, 0)),
        out_shape=jax.ShapeDtypeStruct((t, d), F32),
        scratch_shapes=[pltpu.VMEM((tm, d), BF16)],
        compiler_params=_cparams("parallel", "arbitrary"),
        name="ffn",
    )(x, mod, w_in, w_in, w_out, ln_g, ln_b)


def _in_proj_kernel(x_ref, mod_ref, w_rw_ref, w_mla_ref, w_conv_ref,
                    rw_ref, mla_ref, conv_ref, *, mod_base):
    sh = mod_ref[0, mod_base:mod_base + 1, :]
    sc = mod_ref[0, mod_base + 1:mod_base + 2, :]
    h = (x_ref[...] * (1.0 + sc) + sh).astype(BF16)
    rw_ref[...] = _dot(h, w_rw_ref[0])
    mla_ref[...] = _dot(h, w_mla_ref[0])
    conv_ref[...] = _dot(h, w_conv_ref[0])


def _in_proj(x, mod, w_in, *, layer, cols, mod_base, seq):
    t, d = x.shape
    rw_cols, mla_cols, conv_cols = cols
    tm = min(TOKEN_TILE, seq)
    per_b = seq // tm
    w_mla = w_in[layer:layer + 1, :, rw_cols:rw_cols + mla_cols]
    w_conv = w_in[layer:layer + 1, :, rw_cols + mla_cols:]
    tok = lambda n: pl.BlockSpec((tm, n), lambda i: (i, 0))
    resident = lambda n, l: pl.BlockSpec((1, d, n), lambda i: (l, 0, 0),
                                         pipeline_mode=pl.Buffered(1))
    kern = functools.partial(_in_proj_kernel, mod_base=mod_base)
    return pl.pallas_call(
        kern,
        grid=(t // tm,),
        in_specs=[tok(d), pl.BlockSpec((1, N_MOD, d), lambda i: (i // per_b, 0, 0)),
                  resident(rw_cols, layer), resident(mla_cols, 0), resident(conv_cols, 0)],
        out_specs=[tok(rw_cols), tok(mla_cols), tok(conv_cols)],
        out_shape=[jax.ShapeDtypeStruct((t, n), F32) for n in cols],
        compiler_params=_cparams("parallel"),
        name="in_proj",
    )(x, mod, w_in, w_mla, w_conv)


def _rwkv_kernel(r_ref, k_ref, v_ref, lo_ref, mur_ref, muk_ref, muv_ref, mulo_ref,
                 w0_ref, w2_ref, a0_ref, a2_ref, g2_ref, kk_ref, ka_ref, rk_ref,
                 lng_ref, lnb_ref, o_ref,
                 state_ref, pr_ref, pk_ref, pv_ref, plo_ref, obuf_ref, *, bb, tb):
    j = pl.program_id(1)
    L = RW_CHUNK
    nc = tb // L

    @pl.when(j == 0)
    def _():
        state_ref[...] = jnp.zeros_like(state_ref)
        pr_ref[...] = jnp.zeros_like(pr_ref)
        pk_ref[...] = jnp.zeros_like(pk_ref)
        pv_ref[...] = jnp.zeros_like(pv_ref)
        plo_ref[...] = jnp.zeros_like(plo_ref)

    lane_r = lax.broadcasted_iota(jnp.int32, (LANES, LANES), 0)
    lane_c = lax.broadcasted_iota(jnp.int32, (LANES, LANES), 1)

    def same_block(size):
        return (lane_r // size) == (lane_c // size)

    same_head = same_block(RW_HEAD_DIM).astype(BF16)
    stack_mask = ((lane_r // L) == (lane_c // RW_HEAD_DIM)).astype(F32)
    strict_lower = lane_r > lane_c
    lower = lane_r >= lane_c
    eye = (lane_r == lane_c).astype(F32)
    tri = (lax.broadcasted_iota(jnp.int32, (L, L), 0)
           >= lax.broadcasted_iota(jnp.int32, (L, L), 1)).astype(BF16)
    zeros = jnp.zeros((LANES, LANES), F32)

    def head_sum(x):
        return _dot_split_lhs(x, same_head, 2)

    def stack(x):
        return jnp.concatenate([x, x], axis=0) * stack_mask

    def prologue(b):
        def shift_mix(p_ref, prev_ref, mu_ref):
            p = p_ref[b]
            row = lax.broadcasted_iota(jnp.int32, p.shape, 0)
            prev = jnp.where(row == 0, prev_ref[b, 0:1, :], pltpu.roll(p, 1, axis=0))
            prev_ref[b, 0:1, :] = p[tb - 1:tb, :]
            return p + (prev - p) * mu_ref[...]

        r = shift_mix(r_ref, pr_ref, mur_ref)
        k = shift_mix(k_ref, pk_ref, muk_ref)
        v = shift_mix(v_ref, pv_ref, muv_ref)
        lo = shift_mix(lo_ref, plo_ref, mulo_ref)
        w_lo = lo[:, :RW_DECAY_LORA]
        a_lo = lo[:, RW_DECAY_LORA:RW_DECAY_LORA + RW_AAA_LORA]
        g_lo = lo[:, RW_DECAY_LORA + RW_AAA_LORA:]
        z = w0_ref[...] + _rw_dot(jnp.tanh(w_lo), w2_ref[...])
        softplus_neg_z = jnp.maximum(-z, 0.0) + jnp.log(1.0 + jnp.exp(-jnp.abs(z)))
        log_decay = -jnp.exp(-softplus_neg_z - 0.5)
        a = _sigmoid(a0_ref[...] + _rw_dot(a_lo, a2_ref[...]))
        gate = _rw_dot(_sigmoid(g_lo), g2_ref[...])
        kk = k * kk_ref[...]
        kk = kk / jnp.maximum(jnp.sqrt(head_sum(kk * kk)), 1e-12)
        k = k * (1.0 + (a - 1.0) * ka_ref[...])
        return r, k, v, kk, a, gate, log_decay

    seqs = [prologue(b) for b in range(bb)]
    items = [(b, c) for c in range(nc) for b in range(bb)]
    every = range(len(items))

    def chunk_operands(b, c):
        r, k, v, kk, a, _, log_decay = seqs[b]
        rows = slice(c * L, (c + 1) * L)
        lw = log_decay[rows]
        cum = _dot_split_rhs(tri, lw, 3)
        w_cum = jnp.exp(cum)
        w_inv = jnp.exp(-cum)
        w_prev = jnp.exp(cum - lw)
        w_last = w_cum[L - 1:L, :]
        kk_c = kk[rows]
        return (stack(-kk_c * w_prev), stack(kk_c * a[rows] * w_inv), stack(k[rows] * w_inv),
                stack(r[rows] * w_cum), stack(v[rows]), w_last)

    ops = [chunk_operands(b, c) for b, c in items]
    a2_ = [o[0] for o in ops]
    b2_ = [o[1] for o in ops]
    k2_ = [o[2] for o in ops]
    r2_ = [o[3] for o in ops]
    v2_ = [o[4] for o in ops]
    w_last = [o[5] for o in ops]

    scores = [_dot_nt(jnp.concatenate([a2_[i], r2_[i]], axis=0).astype(BF16),
                      jnp.concatenate([b2_[i], k2_[i]], axis=0).astype(BF16)) for i in every]
    a_ab = [jnp.where(strict_lower, s[:LANES, :LANES], 0.0) for s in scores]
    a_ak = [jnp.where(strict_lower, s[:LANES, LANES:], 0.0) for s in scores]
    a_rb = [jnp.where(lower, s[LANES:, :LANES], 0.0) for s in scores]
    a_rk = [jnp.where(lower, s[LANES:, LANES:], 0.0) for s in scores]

    diag_blocks = same_block(RW_INV_BASE)
    p = [jnp.where(diag_blocks, x, 0.0) for x in a_ab]
    t = [eye + x for x in p]
    n = 2
    while n < RW_INV_BASE:
        p = [_mm3(x, x) for x in p]
        t = [t[i] + _mm3(t[i], p[i]) for i in every]
        n *= 2
    size = 2 * RW_INV_BASE
    while size <= L:
        off_diag = same_block(size) & ~same_block(size // 2)
        et = [_mm3(jnp.where(off_diag, a_ab[i], 0.0), t[i]) for i in every]
        t = [t[i] + _mm3(t[i], et[i]) for i in every]
        size *= 2

    akv = [_rw_dot(a_ak[i], v2_[i]) for i in every]
    x = [_mm3(t[i], jnp.concatenate([a2_[i], akv[i]], axis=1)) for i in every]
    big = []
    for i in every:
        lhs = jnp.concatenate(
            [jnp.concatenate([a_rb[i], a_rk[i]], axis=1),
             jnp.concatenate([(b2_[i] * w_last[i]).T, (k2_[i] * w_last[i]).T], axis=1)], axis=0)
        rhs = jnp.concatenate([x[i], jnp.concatenate([zeros, v2_[i]], axis=1)], axis=0)
        big.append(_mm3(lhs, rhs))

    states = [state_ref[b] for b in range(bb)]
    for i, (b, c) in enumerate(items):
        r_hat = r2_[i] + big[i][:LANES, :LANES]
        st = _rw_dot(jnp.concatenate([r_hat, big[i][LANES:, :LANES]], axis=0), states[b])
        o2 = st[:LANES] + big[i][:LANES, LANES:]
        obuf_ref[b, c * L:(c + 1) * L, :] = o2[:L] + o2[L:]
        w_rows = jnp.broadcast_to(w_last[i], (LANES, LANES)).T
        states[b] = states[b] * w_rows + st[LANES:] + big[i][LANES:, LANES:]
    for b in range(bb):
        state_ref[b] = states[b]

    inv_n = 1.0 / RW_HEAD_DIM
    for b in range(bb):
        r, k, v, _, _, gate, _ = seqs[b]
        o = obuf_ref[b]
        mean = head_sum(o) * inv_n
        d = o - mean
        var = head_sum(d * d) * inv_n
        on = d * lax.rsqrt(var + RW_GN_EPS) * lng_ref[...] + lnb_ref[...]
        bonus = head_sum(r * k * rk_ref[...]) * v
        o_ref[b] = ((on + bonus) * gate).astype(o_ref.dtype)


def _rwkv(p_rw, mu, w0, w2, a0, a2, g2, k_k, k_a, r_k, lnx_g, lnx_b, *, batch, seq):
    width = w0.shape[-1]
    pairs = width // LANES
    tb = min(RW_BLOCK, seq)
    lo_w = RW_DECAY_LORA + RW_AAA_LORA + RW_GATE_LORA
    lo_blk = 3 * width // lo_w
    p3 = p_rw.reshape(batch, seq, p_rw.shape[-1])

    def col(off):
        return lambda h, j: (0, j, off + h)

    def vec(off):
        return lambda h, j: (0, off + h)

    row = lambda a: a.reshape(1, -1)
    kern = functools.partial(_rwkv_kernel, bb=batch, tb=tb)
    out = pl.pallas_call(
        kern,
        grid=(pairs, seq // tb),
        in_specs=[pl.BlockSpec((batch, tb, LANES), col(0)),
                  pl.BlockSpec((batch, tb, LANES), col(pairs)),
                  pl.BlockSpec((batch, tb, LANES), col(2 * pairs)),
                  pl.BlockSpec((batch, tb, lo_w), lambda h, j: (0, j, lo_blk)),
                  pl.BlockSpec((1, LANES), vec(0)),
                  pl.BlockSpec((1, LANES), vec(pairs)),
                  pl.BlockSpec((1, LANES), vec(2 * pairs)),
                  pl.BlockSpec((1, lo_w), lambda h, j: (0, lo_blk)),
                  pl.BlockSpec((1, LANES), vec(0)),
                  pl.BlockSpec((RW_DECAY_LORA, LANES), vec(0)),
                  pl.BlockSpec((1, LANES), vec(0)),
                  pl.BlockSpec((RW_AAA_LORA, LANES), vec(0)),
                  pl.BlockSpec((RW_GATE_LORA, LANES), vec(0)),
                  pl.BlockSpec((1, LANES), vec(0)),
                  pl.BlockSpec((1, LANES), vec(0)),
                  pl.BlockSpec((1, LANES), vec(0)),
                  pl.BlockSpec((1, LANES), vec(0)),
                  pl.BlockSpec((1, LANES), vec(0))],
        out_specs=pl.BlockSpec((batch, tb, LANES), lambda h, j: (0, j, h)),
        out_shape=jax.ShapeDtypeStruct((batch, seq, width), RW_OUT_DTYPE),
        scratch_shapes=[pltpu.VMEM((batch, LANES, LANES), F32),
                        pltpu.VMEM((batch, SUBLANES, LANES), F32),
                        pltpu.VMEM((batch, SUBLANES, LANES), F32),
                        pltpu.VMEM((batch, SUBLANES, LANES), F32),
                        pltpu.VMEM((batch, SUBLANES, lo_w), F32),
                        pltpu.VMEM((batch, tb, LANES), F32)],
        compiler_params=_cparams("parallel", "arbitrary"),
        name="rwkv",
    )(p3, p3, p3, p3, row(mu), row(mu), row(mu), row(mu), row(w0), w2, row(a0), a2, g2,
      row(k_k), row(k_a), row(r_k), row(lnx_g), row(lnx_b))
    return out.reshape(batch * seq, width)


def _rope_table_kernel(pos_ref, freq_ref, cos_ref, sa_ref, sb_ref):
    half = MLA_ROPE // 2
    ang = pos_ref[...].astype(F32) * freq_ref[...]
    lane = lax.broadcasted_iota(jnp.int32, ang.shape, 1)
    cos = jnp.cos(ang)
    sin = jnp.sin(ang)
    cos_ref[...] = jnp.where(lane < MLA_ROPE, cos, 1.0)
    sa_ref[...] = jnp.where(lane < half, -sin, 0.0)
    sb_ref[...] = jnp.where((lane >= half) & (lane < MLA_ROPE), sin, 0.0)


def _rope_tables(positions):
    t = positions.size
    half = MLA_ROPE // 2
    tm = min(2048, t)
    inv_freq = ROPE_THETA ** (-jnp.arange(half, dtype=F32) / half)
    freq = jnp.concatenate([inv_freq, inv_freq, jnp.zeros((LANES - MLA_ROPE,), F32)])
    spec = pl.BlockSpec((tm, LANES), lambda i: (i, 0))
    shp = jax.ShapeDtypeStruct((t, LANES), F32)
    return pl.pallas_call(
        _rope_table_kernel,
        grid=(t // tm,),
        in_specs=[pl.BlockSpec((tm, 1), lambda i: (i, 0)),
                  pl.BlockSpec((1, LANES), lambda i: (0, 0))],
        out_specs=[spec, spec, spec],
        out_shape=[shp, shp, shp],
        compiler_params=_cparams("parallel"),
        name="rope_tables",
    )(positions.reshape(t, 1), freq.reshape(1, LANES))


def _rope(x, cos, sa, sb):
    half = MLA_ROPE // 2
    return x * cos + pltpu.roll(x, LANES - half, axis=1) * sa + pltpu.roll(x, half, axis=1) * sb


def _mla_up_kernel(p_ref, cos_ref, sa_ref, sb_ref, qg_ref, wq_ref, kvg_ref, wk_ref, wvt_ref,
                   q_ref, k_ref, vt_ref, *, heads):
    p = p_ref[...]
    cos, sa, sb = cos_ref[...], sa_ref[...], sb_ref[...]

    def rms(x, g):
        return x * lax.rsqrt(jnp.mean(x * x, -1, keepdims=True) + RMS_EPS) * g

    q_lat = rms(p[:, :MLA_Q_LORA], qg_ref[...]).astype(BF16)
    kv_lat = rms(p[:, MLA_Q_LORA:MLA_Q_LORA + MLA_KV_LORA], kvg_ref[...]).astype(BF16)
    k_pe = p[:, MLA_Q_LORA + MLA_KV_LORA:]
    k_pe = jnp.concatenate([k_pe, jnp.zeros((k_pe.shape[0], LANES - MLA_ROPE), F32)], axis=1)
    k_pe = _rope(k_pe, cos, sa, sb).astype(BF16)

    q = _dot(q_lat, wq_ref[...]) * (MLA_SCALE * LOG2_E)
    k_nope = _dot(kv_lat, wk_ref[...])
    vt_ref[0] = _dot_nt(wvt_ref[...], kv_lat).astype(BF16)
    for h in range(heads):
        base = h * MLA_QK_PAD
        q_ref[:, base:base + MLA_NOPE] = q[:, base:base + MLA_NOPE].astype(BF16)
        q_ref[:, base + MLA_NOPE:base + MLA_QK_PAD] = _rope(
            q[:, base + MLA_NOPE:base + MLA_QK_PAD], cos, sa, sb).astype(BF16)
        k_ref[:, base:base + MLA_NOPE] = k_nope[:, h * MLA_NOPE:(h + 1) * MLA_NOPE].astype(BF16)
        k_ref[:, base + MLA_NOPE:base + MLA_QK_PAD] = k_pe


def _mla_up(p_mla, tables, q_norm_g, w_uq, kv_norm_g, w_ukv, *, seq):
    t, cols = p_mla.shape
    heads = w_uq.shape[1] // (MLA_NOPE + MLA_ROPE)
    tm = min(TOKEN_TILE, seq)
    wq = w_uq.reshape(MLA_Q_LORA, heads, MLA_NOPE + MLA_ROPE)
    wq = jnp.pad(wq, ((0, 0), (0, 0), (0, MLA_QK_PAD - MLA_NOPE - MLA_ROPE)))
    wq = wq.reshape(MLA_Q_LORA, heads * MLA_QK_PAD).astype(BF16)
    wkv = w_ukv.reshape(MLA_KV_LORA, heads, MLA_NOPE + MLA_V)
    wk = wkv[:, :, :MLA_NOPE].reshape(MLA_KV_LORA, heads * MLA_NOPE).astype(BF16)
    wvt = wkv[:, :, MLA_NOPE:].reshape(MLA_KV_LORA, heads * MLA_V).T.astype(BF16)
    tok = lambda n: pl.BlockSpec((tm, n), lambda i: (i, 0))
    full = lambda a: pl.BlockSpec(a.shape, lambda i: (0, 0))
    qg = q_norm_g.reshape(1, -1)
    kvg = kv_norm_g.reshape(1, -1)
    kern = functools.partial(_mla_up_kernel, heads=heads)
    return pl.pallas_call(
        kern,
        grid=(t // tm,),
        in_specs=[tok(cols), tok(LANES), tok(LANES), tok(LANES),
                  full(qg), full(wq), full(kvg), full(wk), full(wvt)],
        out_specs=[tok(heads * MLA_QK_PAD), tok(heads * MLA_QK_PAD),
                   pl.BlockSpec((1, heads * MLA_V, tm), lambda i: (i, 0, 0))],
        out_shape=[jax.ShapeDtypeStruct((t, heads * MLA_QK_PAD), BF16),
                   jax.ShapeDtypeStruct((t, heads * MLA_QK_PAD), BF16),
                   jax.ShapeDtypeStruct((t // tm, heads * MLA_V, tm), BF16)],
        compiler_params=_cparams("parallel"),
        name="mla_up",
    )(p_mla, *tables, qg, wq, kvg, wk, wvt)


def _attn_kernel(qi_ref, kj_ref, q_ref, k_ref, vt_ref, o_ref, m_ref, l_ref, acc_ref, *, heads):
    step_id = pl.program_id(1)
    i = qi_ref[step_id]
    j = kj_ref[step_id]

    @pl.when(j == 0)
    def _():
        m_ref[...] = jnp.full_like(m_ref, -jnp.inf)
        l_ref[...] = jnp.zeros_like(l_ref)
        acc_ref[...] = jnp.zeros_like(acc_ref)

    def qk(h):
        return slice(h * MLA_QK_PAD, (h + 1) * MLA_QK_PAD)

    def vo(h):
        return slice(h * MLA_V, (h + 1) * MLA_V)

    def step(masked):
        hs = range(heads)
        s = [_dot_nt(k_ref[:, qk(h)], q_ref[:, qk(h)]) for h in hs]
        if masked:
            key = lax.broadcasted_iota(jnp.int32, s[0].shape, 0)
            query = lax.broadcasted_iota(jnp.int32, s[0].shape, 1)
            s = [jnp.where(key <= query, x, -jnp.inf) for x in s]
        m_prev = [m_ref[h] for h in hs]
        m_new = [jnp.maximum(m_prev[h], jnp.max(s[h], 0, keepdims=True)) for h in hs]
        alpha = [jnp.exp2(m_prev[h] - m_new[h]) for h in hs]
        p = [jnp.exp2(s[h] - m_new[h]) for h in hs]
        pv = [_dot(vt_ref[0, vo(h), :], p[h].astype(BF16)) for h in hs]
        for h in hs:
            l_ref[h] = alpha[h] * l_ref[h] + jnp.sum(p[h], 0, keepdims=True)
            acc_ref[vo(h), :] = alpha[h] * acc_ref[vo(h), :] + pv[h]
            m_ref[h] = m_new[h]

    @pl.when(j < i)
    def _():
        step(False)

    @pl.when(j == i)
    def _():
        step(True)
        for h in range(heads):
            o_ref[:, vo(h)] = (acc_ref[vo(h), :] / l_ref[h]).T.astype(o_ref.dtype)


def _attention(q, k, vt, *, batch, seq):
    t = q.shape[0]
    heads = vt.shape[1] // MLA_V
    tq = min(ATTN_TILE, seq)
    assert vt.shape[2] == tq
    nq = seq // tq
    pairs = [(i, j) for i in range(nq) for j in range(i + 1)]
    qi = jnp.asarray([p[0] for p in pairs], jnp.int32)
    kj = jnp.asarray([p[1] for p in pairs], jnp.int32)
    kern = functools.partial(_attn_kernel, heads=heads)
    grid_spec = pltpu.PrefetchScalarGridSpec(
        num_scalar_prefetch=2,
        grid=(batch, len(pairs)),
        in_specs=[pl.BlockSpec((tq, heads * MLA_QK_PAD), lambda b, s, qi, kj: (b * nq + qi[s], 0)),
                  pl.BlockSpec((tq, heads * MLA_QK_PAD), lambda b, s, qi, kj: (b * nq + kj[s], 0)),
                  pl.BlockSpec((1, heads * MLA_V, tq),
                               lambda b, s, qi, kj: (b * nq + kj[s], 0, 0))],
        out_specs=pl.BlockSpec((tq, heads * MLA_V), lambda b, s, qi, kj: (b * nq + qi[s], 0)),
        scratch_shapes=[pltpu.VMEM((heads, 1, tq), F32), pltpu.VMEM((heads, 1, tq), F32),
                        pltpu.VMEM((heads * MLA_V, tq), F32)])
    return pl.pallas_call(
        kern,
        grid_spec=grid_spec,
        out_shape=jax.ShapeDtypeStruct((t, heads * MLA_V), BF16),
        compiler_params=_cparams("parallel", "arbitrary"),
        name="attention",
    )(qi, kj, q, k, vt)


def _conv_kernel(lin_ref, gate_ref, w_ref, b_ref, lng_ref, lnb_ref, o_ref, u_ref, *, tm):
    j = pl.program_id(1)

    @pl.when(j == 0)
    def _():
        u_ref[0:CONV_HALO, :] = jnp.zeros((CONV_HALO, u_ref.shape[1]), F32)

    u_ref[CONV_HALO:, :] = lin_ref[0] * _sigmoid(gate_ref[0])
    first = CONV_HALO - (CONV_K - 1)
    u = u_ref[...]
    rows = u.shape[0]
    acc = jnp.zeros((tm, u.shape[1]), F32)
    for phase in range(SUBLANES):
        shifted = u if phase == 0 else pltpu.roll(u, rows - phase, axis=0)
        for base in range(0, CONV_HALO + 1, SUBLANES):
            tap = base + phase - first
            if 0 <= tap < CONV_K:
                acc = acc + shifted[base:base + tm, :] * w_ref[tap:tap + 1, :]
    halo = u_ref[tm:tm + CONV_HALO, :]
    u_ref[0:CONV_HALO, :] = halo
    y = _layer_norm(acc + b_ref[...], lng_ref[...], lnb_ref[...])
    o_ref[0] = _silu(y).astype(o_ref.dtype)


def _conv(p_conv, conv_w, conv_b, ln_g, ln_b, *, batch, seq):
    ch = conv_w.shape[1]
    tm = min(TOKEN_TILE, seq)
    p3 = p_conv.reshape(batch, seq, 2 * ch)
    full = lambda a: pl.BlockSpec(a.shape, lambda b, j: (0, 0))
    row = lambda a: a.reshape(1, -1)
    kern = functools.partial(_conv_kernel, tm=tm)
    out = pl.pallas_call(
        kern,
        grid=(batch, seq // tm),
        in_specs=[pl.BlockSpec((1, tm, ch), lambda b, j: (b, j, 0)),
                  pl.BlockSpec((1, tm, ch), lambda b, j: (b, j, 1)),
                  full(conv_w), full(row(conv_b)), full(row(ln_g)), full(row(ln_b))],
        out_specs=pl.BlockSpec((1, tm, ch), lambda b, j: (b, j, 0)),
        out_shape=jax.ShapeDtypeStruct((batch, seq, ch), BF16),
        scratch_shapes=[pltpu.VMEM((CONV_HALO + tm, ch), F32)],
        compiler_params=_cparams("parallel", "arbitrary"),
        name="conv",
    )(p3, p3, conv_w, row(conv_b), row(ln_g), row(ln_b))
    return out.reshape(batch * seq, ch)


def _out_proj_kernel(x_ref, mod_ref, y1_ref, y2_ref, y3_ref, w1_ref, w2_ref, w3_ref,
                     lng_ref, lnb_ref, o_ref, *, mod_base, alpha):
    y = (_dot(y1_ref[...], w1_ref[0]) + _dot(y2_ref[...], w2_ref[0])
         + _dot(y3_ref[...], w3_ref[0]))
    g = mod_ref[0, mod_base + 2:mod_base + 3, :]
    o_ref[...] = _layer_norm(alpha * x_ref[...] + (1.0 + g) * y, lng_ref[...], lnb_ref[...])


def _out_proj(x, mod, ys, w_out, ln_g, ln_b, *, layer, mod_base, alpha, seq):
    t, d = x.shape
    tm = min(TOKEN_TILE, seq)
    per_b = seq // tm
    tok = lambda n: pl.BlockSpec((tm, n), lambda i: (i, 0))
    full = lambda a: pl.BlockSpec(a.shape, lambda i: (0, 0))
    w_specs = []
    offset = 0
    for y in ys:
        width = y.shape[1]
        assert offset % width == 0
        w_specs.append(pl.BlockSpec((1, width, d), functools.partial(
            lambda i, blk: (layer, blk, 0), blk=offset // width)))
        offset += width
    assert offset == w_out.shape[1]
    kern = functools.partial(_out_proj_kernel, mod_base=mod_base, alpha=alpha)
    return pl.pallas_call(
        kern,
        grid=(t // tm,),
        in_specs=[tok(d), pl.BlockSpec((1, N_MOD, d), lambda i: (i // per_b, 0, 0)),
                  *[tok(y.shape[1]) for y in ys], *w_specs, full(ln_g), full(ln_b)],
        out_specs=tok(d),
        out_shape=jax.ShapeDtypeStruct((t, d), F32),
        compiler_params=_cparams("parallel"),
        name="out_proj",
    )(x, mod, *ys, w_out, w_out, w_out, ln_g, ln_b)


def kernel(x, c, positions, w_ada, b_ada, ln_g, ln_b, w_ffn1_in, w_ffn1_out, w_ffn2_in, w_ffn2_out, w_in, w_out, rw_mu, rw_w0, rw_w2, rw_a0, rw_a2, rw_g2, rw_k_k, rw_k_a, rw_r_k, rw_lnx_g, rw_lnx_b, mla_q_norm_g, mla_w_uq, mla_kv_norm_g, mla_w_ukv, conv_w, conv_b, conv_ln_g, conv_ln_b):
    batch, seq, d = x.shape
    depth = w_ada.shape[0]
    alpha = (2 * depth) ** 0.25
    rw_cols = rw_mu.shape[1]
    mla_cols = MLA_Q_LORA + MLA_KV_LORA + MLA_ROPE
    conv_cols = w_in.shape[2] - rw_cols - mla_cols

    mod_all = _ada(c, w_ada, b_ada).reshape(depth, batch, N_MOD, d)
    tables = _rope_tables(positions)
    xt = x.reshape(batch * seq, d)
    row = lambda a: a.reshape(1, -1)
    w_ffn1_out, w_ffn2_out, w_in, w_out = (
        w.astype(BF16) for w in (w_ffn1_out, w_ffn2_out, w_in, w_out))

    def column_tiles(w):
        tiles = w.astype(BF16).reshape(depth, d, w.shape[2] // FFN_TILE, FFN_TILE)
        return tiles.transpose(0, 2, 1, 3)

    w_ffn1_in, w_ffn2_in = column_tiles(w_ffn1_in), column_tiles(w_ffn2_in)

    for l in range(depth):
        mod = mod_all[l]
        xt = _ffn(xt, mod, w_ffn1_in, w_ffn1_out, row(ln_g[l, 0]), row(ln_b[l, 0]),
                  layer=l, mod_base=0, alpha=alpha, seq=seq)

        p_rw, p_mla, p_conv = _in_proj(xt, mod, w_in, layer=l,
                                       cols=(rw_cols, mla_cols, conv_cols), mod_base=3, seq=seq)

        y_rw = _rwkv(p_rw, rw_mu[l], rw_w0[l], rw_w2[l], rw_a0[l], rw_a2[l], rw_g2[l],
                     rw_k_k[l], rw_k_a[l], rw_r_k[l], rw_lnx_g[l], rw_lnx_b[l],
                     batch=batch, seq=seq)
        q, k, v = _mla_up(p_mla, tables, mla_q_norm_g[l], mla_w_uq[l], mla_kv_norm_g[l],
                          mla_w_ukv[l], seq=seq)
        y_mla = _attention(q, k, v, batch=batch, seq=seq)
        y_conv = _conv(p_conv, conv_w[l], conv_b[l], conv_ln_g[l], conv_ln_b[l],
                       batch=batch, seq=seq)

        xt = _out_proj(xt, mod, (y_rw, y_mla, y_conv), w_out, row(ln_g[l, 1]), row(ln_b[l, 1]),
                       layer=l, mod_base=3, alpha=alpha, seq=seq)

        xt = _ffn(xt, mod, w_ffn2_in, w_ffn2_out, row(ln_g[l, 2]), row(ln_b[l, 2]),
                  layer=l, mod_base=6, alpha=alpha, seq=seq)
    return xt.reshape(batch, seq, d)
```

```python
import functools

import jax
import jax.numpy as jnp
from jax import lax
from jax.experimental import pallas as pl
from jax.experimental.pallas import tpu as pltpu

F32 = jnp.float32
BF16 = jnp.bfloat16

LANES = 128
SUBLANES = 8
RW_HEAD_DIM = 64
RW_CHUNK = 64
RW_INV_BASE = 16
RW_OUT_DTYPE = jnp.bfloat16
RW_DECAY_LORA = 64
RW_AAA_LORA = 64
RW_GATE_LORA = 128
RW_GN_EPS = 64e-5
MLA_NOPE = 128
MLA_ROPE = 64
MLA_V = 128
MLA_Q_LORA = 384
MLA_KV_LORA = 256
MLA_QK_PAD = 256
MLA_SCALE = (MLA_NOPE + MLA_ROPE) ** -0.5
LOG2_E = 1.4426950408889634
ROPE_THETA = 10000.0
CONV_K = 31
CONV_HALO = 32
N_MOD = 9
FFN_RES = 0.5
LN_EPS = 1e-5
RMS_EPS = 1e-6
VMEM_LIMIT = 52 * 1024 * 1024
TOKEN_TILE = 512
ATTN_TILE = 512
ATTN_Q_TILES = 2
RW_BLOCK = 512
FFN_TILE = 512
FFN_TOKEN_TILE = 512


def _cparams(*sem):
    return pltpu.CompilerParams(dimension_semantics=sem, vmem_limit_bytes=VMEM_LIMIT)


def _dot(a, b):
    return jnp.dot(a, b, preferred_element_type=F32)


def _dot_nt(a, b):
    return lax.dot_general(a, b, (((1,), (1,)), ((), ())), preferred_element_type=F32)


def _rw_dot(a, b):
    return _dot(a.astype(BF16), b.astype(BF16))


def _split_bf16(x, terms):
    parts = []
    for _ in range(terms):
        hi = x.astype(BF16)
        parts.append(hi)
        x = x - hi.astype(F32)
    return parts


def _dot_split_lhs(x, exact_rhs, terms):
    return sum(_dot(part, exact_rhs) for part in _split_bf16(x, terms))


def _dot_split_rhs(exact_lhs, x, terms):
    return sum(_dot(exact_lhs, part) for part in _split_bf16(x, terms))


def _mm3(a, b):
    a_hi, a_lo = _split_bf16(a, 2)
    b_hi, b_lo = _split_bf16(b, 2)
    k, n = b.shape
    if k != LANES:
        return _dot(a_hi, b_hi) + _dot(a_lo, b_hi) + _dot(a_hi, b_lo)
    lhs = jnp.concatenate([a_hi, a_lo], axis=1)
    if n != LANES:
        return _dot(lhs, jnp.concatenate([b_hi, b_hi], axis=0)) + _dot(a_hi, b_lo)
    rhs = jnp.concatenate([jnp.concatenate([b_hi, b_lo], axis=1),
                           jnp.concatenate([b_hi, jnp.zeros_like(b_lo)], axis=1)], axis=0)
    out = _dot(lhs, rhs)
    return out[:, :n] + out[:, n:]


def _sigmoid(x):
    return 1.0 / (1.0 + jnp.exp(-x))


def _silu(x):
    return x * _sigmoid(x)


def _layer_norm(y, g, b):
    mean = jnp.mean(y, -1, keepdims=True)
    d = y - mean
    var = jnp.mean(d * d, -1, keepdims=True)
    return d * lax.rsqrt(var + LN_EPS) * g + b


def _ada_kernel(ct_ref, w_ref, b_ref, o_ref, *, batch):
    w = w_ref[0]
    rows = [jnp.sum(_silu(ct_ref[:, b:b + 1]) * w, axis=0, keepdims=True) for b in range(batch)]
    o_ref[0] = jnp.concatenate(rows, axis=0) + b_ref[0]


def _ada(c, w_ada, b_ada):
    depth, d, n = w_ada.shape
    batch = c.shape[0]
    tn = 1024
    kern = functools.partial(_ada_kernel, batch=batch)
    return pl.pallas_call(
        kern,
        grid=(depth, n // tn),
        in_specs=[pl.BlockSpec((d, batch), lambda l, j: (0, 0)),
                  pl.BlockSpec((1, d, tn), lambda l, j: (l, 0, j)),
                  pl.BlockSpec((1, 1, tn), lambda l, j: (l, 0, j))],
        out_specs=pl.BlockSpec((1, batch, tn), lambda l, j: (l, 0, j)),
        out_shape=jax.ShapeDtypeStruct((depth, batch, n), F32),
        compiler_params=_cparams("parallel", "parallel"),
        name="ada",
    )(c.T, w_ada, b_ada.reshape(depth, 1, n))


def _ffn_kernel(x_ref, mod_ref, wg_ref, wu_ref, wo_ref, lng_ref, lnb_ref, o_ref,
                h_ref, *, mod_base, alpha):
    j = pl.program_id(1)
    last = pl.num_programs(1) - 1

    def contribution(h):
        gate = _dot(h, wg_ref[0])
        up = _dot(h, wu_ref[0])
        act = (_silu(gate) * up).astype(BF16)
        return _dot(act, wo_ref[0])

    @pl.when(j == 0)
    def _():
        sh = mod_ref[0, mod_base:mod_base + 1, :]
        sc = mod_ref[0, mod_base + 1:mod_base + 2, :]
        h = (x_ref[...] * (1.0 + sc) + sh).astype(BF16)
        h_ref[...] = h
        o_ref[...] = contribution(h)

    @pl.when((j > 0) & (j < last))
    def _():
        o_ref[...] += contribution(h_ref[...])

    @pl.when(j == last)
    def _():
        acc = o_ref[...] + contribution(h_ref[...])
        g = mod_ref[0, mod_base + 2:mod_base + 3, :]
        y = alpha * x_ref[...] + (FFN_RES * (1.0 + g)) * acc
        o_ref[...] = _layer_norm(y, lng_ref[...], lnb_ref[...])


def _ffn(x, mod, w_in, w_out, ln_g, ln_b, *, layer, mod_base, alpha, seq):
    t, d = x.shape
    f = w_out.shape[1]
    tm = min(FFN_TOKEN_TILE, seq)
    tf = FFN_TILE
    nf = f // tf
    assert nf >= 2
    per_b = seq // tm
    kern = functools.partial(_ffn_kernel, mod_base=mod_base, alpha=alpha)
    return pl.pallas_call(
        kern,
        grid=(t // tm, nf),
        in_specs=[pl.BlockSpec((tm, d), lambda i, j: (i, 0)),
                  pl.BlockSpec((1, N_MOD, d), lambda i, j: (i // per_b, 0, 0)),
                  pl.BlockSpec((1, d, tf), lambda i, j: (layer, 0, j)),
                  pl.BlockSpec((1, d, tf), lambda i, j: (layer, 0, nf + j)),
                  pl.BlockSpec((1, tf, d), lambda i, j: (layer, j, 0)),
                  pl.BlockSpec((1, d), lambda i, j: (0, 0)),
                  pl.BlockSpec((1, d), lambda i, j: (0, 0))],
        out_specs=pl.BlockSpec((tm, d), lambda i, j: (i, 0)),
        out_shape=jax.ShapeDtypeStruct((t, d), F32),
        scratch_shapes=[pltpu.VMEM((tm, d), BF16)],
        compiler_params=_cparams("parallel", "arbitrary"),
        name="ffn",
    )(x, mod, w_in, w_in, w_out, ln_g, ln_b)


def _conv_branch(p_conv, u_ref, w_ref, b_ref, lng_ref, lnb_ref):
    ch = u_ref.shape[1]
    tm = p_conv.shape[0]
    u_ref[CONV_HALO:, :] = p_conv[:, :ch] * _sigmoid(p_conv[:, ch:])
    first = CONV_HALO - (CONV_K - 1)
    u = u_ref[...]
    rows = u.shape[0]
    acc = jnp.zeros((tm, ch), F32)
    for phase in range(SUBLANES):
        shifted = u if phase == 0 else pltpu.roll(u, rows - phase, axis=0)
        for base in range(0, CONV_HALO + 1, SUBLANES):
            tap = base + phase - first
            if 0 <= tap < CONV_K:
                acc = acc + shifted[base:base + tm, :] * w_ref[tap:tap + 1, :]
    halo = u_ref[tm:tm + CONV_HALO, :]
    u_ref[0:CONV_HALO, :] = halo
    return _silu(_layer_norm(acc + b_ref[...], lng_ref[...], lnb_ref[...]))


def _in_proj_kernel(x_ref, mod_ref, w_rw_ref, w_mla_ref, w_conv_ref,
                    cw_ref, cb_ref, clng_ref, clnb_ref,
                    rw_ref, mla_ref, yconv_ref, u_ref, *, mod_base, per_b):
    @pl.when(pl.program_id(0) % per_b == 0)
    def _():
        u_ref[0:CONV_HALO, :] = jnp.zeros((CONV_HALO, u_ref.shape[1]), F32)

    sh = mod_ref[0, mod_base:mod_base + 1, :]
    sc = mod_ref[0, mod_base + 1:mod_base + 2, :]
    h = (x_ref[...] * (1.0 + sc) + sh).astype(BF16)
    p_conv = _dot(h, w_conv_ref[0])
    rw_ref[...] = _dot(h, w_rw_ref[0])
    mla_ref[...] = _dot(h, w_mla_ref[0])
    yconv_ref[...] = _conv_branch(p_conv, u_ref, cw_ref, cb_ref, clng_ref,
                                  clnb_ref).astype(yconv_ref.dtype)


def _in_proj(x, mod, w_in, conv_w, conv_b, conv_ln_g, conv_ln_b, *, layer, cols, mod_base, seq):
    t, d = x.shape
    rw_cols, mla_cols, conv_cols = cols
    ch = conv_w.shape[1]
    assert conv_cols == 2 * ch
    tm = min(TOKEN_TILE, seq)
    per_b = seq // tm
    w_mla = w_in[layer:layer + 1, :, rw_cols:rw_cols + mla_cols]
    w_conv = w_in[layer:layer + 1, :, rw_cols + mla_cols:]
    tok = lambda n: pl.BlockSpec((tm, n), lambda i: (i, 0))
    full = lambda a: pl.BlockSpec(a.shape, lambda i: (0, 0))
    resident = lambda n, l: pl.BlockSpec((1, d, n), lambda i: (l, 0, 0),
                                         pipeline_mode=pl.Buffered(1))
    row = lambda a: a.reshape(1, -1)
    small = (conv_w, row(conv_b), row(conv_ln_g), row(conv_ln_b))
    kern = functools.partial(_in_proj_kernel, mod_base=mod_base, per_b=per_b)
    return pl.pallas_call(
        kern,
        grid=(t // tm,),
        in_specs=[tok(d), pl.BlockSpec((1, N_MOD, d), lambda i: (i // per_b, 0, 0)),
                  resident(rw_cols, layer), resident(mla_cols, 0), resident(conv_cols, 0),
                  *[full(a) for a in small]],
        out_specs=[tok(rw_cols), tok(mla_cols), tok(ch)],
        out_shape=[jax.ShapeDtypeStruct((t, rw_cols), F32),
                   jax.ShapeDtypeStruct((t, mla_cols), F32),
                   jax.ShapeDtypeStruct((t, ch), BF16)],
        scratch_shapes=[pltpu.VMEM((CONV_HALO + tm, ch), F32)],
        compiler_params=_cparams("arbitrary"),
        name="in_proj",
    )(x, mod, w_in, w_mla, w_conv, *small)


def _rwkv_kernel(r_ref, k_ref, v_ref, lo_ref, mur_ref, muk_ref, muv_ref, mulo_ref,
                 w0_ref, w2_ref, a0_ref, a2_ref, g2_ref, kk_ref, ka_ref, rk_ref,
                 lng_ref, lnb_ref, o_ref,
                 state_ref, pr_ref, pk_ref, pv_ref, plo_ref, obuf_ref, *, bb, tb):
    j = pl.program_id(1)
    L = RW_CHUNK
    nc = tb // L

    @pl.when(j == 0)
    def _():
        state_ref[...] = jnp.zeros_like(state_ref)
        pr_ref[...] = jnp.zeros_like(pr_ref)
        pk_ref[...] = jnp.zeros_like(pk_ref)
        pv_ref[...] = jnp.zeros_like(pv_ref)
        plo_ref[...] = jnp.zeros_like(plo_ref)

    lane_r = lax.broadcasted_iota(jnp.int32, (LANES, LANES), 0)
    lane_c = lax.broadcasted_iota(jnp.int32, (LANES, LANES), 1)

    def same_block(size):
        return (lane_r // size) == (lane_c // size)

    same_head = same_block(RW_HEAD_DIM).astype(BF16)
    stack_mask = ((lane_r // L) == (lane_c // RW_HEAD_DIM)).astype(F32)
    strict_lower = lane_r > lane_c
    lower = lane_r >= lane_c
    eye = (lane_r == lane_c).astype(F32)
    tri = (lax.broadcasted_iota(jnp.int32, (L, L), 0)
           >= lax.broadcasted_iota(jnp.int32, (L, L), 1)).astype(BF16)
    zeros = jnp.zeros((LANES, LANES), F32)

    def head_sum(x):
        return _dot_split_lhs(x, same_head, 2)

    def stack(x):
        return jnp.concatenate([x, x], axis=0) * stack_mask

    def prologue(b):
        def shift_mix(p_ref, prev_ref, mu_ref):
            p = p_ref[b]
            row = lax.broadcasted_iota(jnp.int32, p.shape, 0)
            prev = jnp.where(row == 0, prev_ref[b, 0:1, :], pltpu.roll(p, 1, axis=0))
            prev_ref[b, 0:1, :] = p[tb - 1:tb, :]
            return p + (prev - p) * mu_ref[...]

        r = shift_mix(r_ref, pr_ref, mur_ref)
        k = shift_mix(k_ref, pk_ref, muk_ref)
        v = shift_mix(v_ref, pv_ref, muv_ref)
        lo = shift_mix(lo_ref, plo_ref, mulo_ref)
        w_lo = lo[:, :RW_DECAY_LORA]
        a_lo = lo[:, RW_DECAY_LORA:RW_DECAY_LORA + RW_AAA_LORA]
        g_lo = lo[:, RW_DECAY_LORA + RW_AAA_LORA:]
        z = w0_ref[...] + _rw_dot(jnp.tanh(w_lo), w2_ref[...])
        softplus_neg_z = jnp.maximum(-z, 0.0) + jnp.log(1.0 + jnp.exp(-jnp.abs(z)))
        log_decay = -jnp.exp(-softplus_neg_z - 0.5)
        a = _sigmoid(a0_ref[...] + _rw_dot(a_lo, a2_ref[...]))
        gate = _rw_dot(_sigmoid(g_lo), g2_ref[...])
        kk = k * kk_ref[...]
        kk = kk / jnp.maximum(jnp.sqrt(head_sum(kk * kk)), 1e-12)
        k = k * (1.0 + (a - 1.0) * ka_ref[...])
        return r, k, v, kk, a, gate, log_decay

    seqs = [prologue(b) for b in range(bb)]
    items = [(b, c) for c in range(nc) for b in range(bb)]
    every = range(len(items))

    def chunk_operands(b, c):
        r, k, v, kk, a, _, log_decay = seqs[b]
        rows = slice(c * L, (c + 1) * L)
        lw = log_decay[rows]
        cum = _dot_split_rhs(tri, lw, 3)
        w_cum = jnp.exp(cum)
        w_inv = jnp.exp(-cum)
        w_prev = jnp.exp(cum - lw)
        w_last = w_cum[L - 1:L, :]
        kk_c = kk[rows]
        return (stack(-kk_c * w_prev), stack(kk_c * a[rows] * w_inv), stack(k[rows] * w_inv),
                stack(r[rows] * w_cum), stack(v[rows]), w_last)

    ops = [chunk_operands(b, c) for b, c in items]
    a2_ = [o[0] for o in ops]
    b2_ = [o[1] for o in ops]
    k2_ = [o[2] for o in ops]
    r2_ = [o[3] for o in ops]
    v2_ = [o[4] for o in ops]
    w_last = [o[5] for o in ops]

    scores = [_dot_nt(jnp.concatenate([a2_[i], r2_[i]], axis=0).astype(BF16),
                      jnp.concatenate([b2_[i], k2_[i]], axis=0).astype(BF16)) for i in every]
    a_ab = [jnp.where(strict_lower, s[:LANES, :LANES], 0.0) for s in scores]
    a_ak = [jnp.where(strict_lower, s[:LANES, LANES:], 0.0) for s in scores]
    a_rb = [jnp.where(lower, s[LANES:, :LANES], 0.0) for s in scores]
    a_rk = [jnp.where(lower, s[LANES:, LANES:], 0.0) for s in scores]

    diag_blocks = same_block(RW_INV_BASE)
    p = [jnp.where(diag_blocks, x, 0.0) for x in a_ab]
    t = [eye + x for x in p]
    n = 2
    while n < RW_INV_BASE:
        p = [_mm3(x, x) for x in p]
        t = [t[i] + _mm3(t[i], p[i]) for i in every]
        n *= 2
    size = 2 * RW_INV_BASE
    while size <= L:
        off_diag = same_block(size) & ~same_block(size // 2)
        et = [_mm3(jnp.where(off_diag, a_ab[i], 0.0), t[i]) for i in every]
        t = [t[i] + _mm3(t[i], et[i]) for i in every]
        size *= 2

    akv = [_rw_dot(a_ak[i], v2_[i]) for i in every]
    x = [_mm3(t[i], jnp.concatenate([a2_[i], akv[i]], axis=1)) for i in every]
    big = []
    for i in every:
        lhs = jnp.concatenate(
            [jnp.concatenate([a_rb[i], a_rk[i]], axis=1),
             jnp.concatenate([(b2_[i] * w_last[i]).T, (k2_[i] * w_last[i]).T], axis=1)], axis=0)
        rhs = jnp.concatenate([x[i], jnp.concatenate([zeros, v2_[i]], axis=1)], axis=0)
        big.append(_mm3(lhs, rhs))

    states = [state_ref[b] for b in range(bb)]
    for i, (b, c) in enumerate(items):
        r_hat = r2_[i] + big[i][:LANES, :LANES]
        st = _rw_dot(jnp.concatenate([r_hat, big[i][LANES:, :LANES]], axis=0), states[b])
        o2 = st[:LANES] + big[i][:LANES, LANES:]
        obuf_ref[b, c * L:(c + 1) * L, :] = o2[:L] + o2[L:]
        w_rows = jnp.broadcast_to(w_last[i], (LANES, LANES)).T
        states[b] = states[b] * w_rows + st[LANES:] + big[i][LANES:, LANES:]
    for b in range(bb):
        state_ref[b] = states[b]

    inv_n = 1.0 / RW_HEAD_DIM
    for b in range(bb):
        r, k, v, _, _, gate, _ = seqs[b]
        o = obuf_ref[b]
        mean = head_sum(o) * inv_n
        d = o - mean
        var = head_sum(d * d) * inv_n
        on = d * lax.rsqrt(var + RW_GN_EPS) * lng_ref[...] + lnb_ref[...]
        bonus = head_sum(r * k * rk_ref[...]) * v
        o_ref[b] = ((on + bonus) * gate).astype(o_ref.dtype)


def _rwkv(p_rw, mu, w0, w2, a0, a2, g2, k_k, k_a, r_k, lnx_g, lnx_b, *, batch, seq):
    width = w0.shape[-1]
    pairs = width // LANES
    tb = min(RW_BLOCK, seq)
    lo_w = RW_DECAY_LORA + RW_AAA_LORA + RW_GATE_LORA
    lo_blk = 3 * width // lo_w
    p3 = p_rw.reshape(batch, seq, p_rw.shape[-1])

    def col(off):
        return lambda h, j: (0, j, off + h)

    def vec(off):
        return lambda h, j: (0, off + h)

    row = lambda a: a.reshape(1, -1)
    kern = functools.partial(_rwkv_kernel, bb=batch, tb=tb)
    out = pl.pallas_call(
        kern,
        grid=(pairs, seq // tb),
        in_specs=[pl.BlockSpec((batch, tb, LANES), col(0)),
                  pl.BlockSpec((batch, tb, LANES), col(pairs)),
                  pl.BlockSpec((batch, tb, LANES), col(2 * pairs)),
                  pl.BlockSpec((batch, tb, lo_w), lambda h, j: (0, j, lo_blk)),
                  pl.BlockSpec((1, LANES), vec(0)),
                  pl.BlockSpec((1, LANES), vec(pairs)),
                  pl.BlockSpec((1, LANES), vec(2 * pairs)),
                  pl.BlockSpec((1, lo_w), lambda h, j: (0, lo_blk)),
                  pl.BlockSpec((1, LANES), vec(0)),
                  pl.BlockSpec((RW_DECAY_LORA, LANES), vec(0)),
                  pl.BlockSpec((1, LANES), vec(0)),
                  pl.BlockSpec((RW_AAA_LORA, LANES), vec(0)),
                  pl.BlockSpec((RW_GATE_LORA, LANES), vec(0)),
                  pl.BlockSpec((1, LANES), vec(0)),
                  pl.BlockSpec((1, LANES), vec(0)),
                  pl.BlockSpec((1, LANES), vec(0)),
                  pl.BlockSpec((1, LANES), vec(0)),
                  pl.BlockSpec((1, LANES), vec(0))],
        out_specs=pl.BlockSpec((batch, tb, LANES), lambda h, j: (0, j, h)),
        out_shape=jax.ShapeDtypeStruct((batch, seq, width), RW_OUT_DTYPE),
        scratch_shapes=[pltpu.VMEM((batch, LANES, LANES), F32),
                        pltpu.VMEM((batch, SUBLANES, LANES), F32),
                        pltpu.VMEM((batch, SUBLANES, LANES), F32),
                        pltpu.VMEM((batch, SUBLANES, LANES), F32),
                        pltpu.VMEM((batch, SUBLANES, lo_w), F32),
                        pltpu.VMEM((batch, tb, LANES), F32)],
        compiler_params=_cparams("parallel", "arbitrary"),
        name="rwkv",
    )(p3, p3, p3, p3, row(mu), row(mu), row(mu), row(mu), row(w0), w2, row(a0), a2, g2,
      row(k_k), row(k_a), row(r_k), row(lnx_g), row(lnx_b))
    return out.reshape(batch * seq, width)


def _rope_table_kernel(pos_ref, freq_ref, cos_ref, sa_ref, sb_ref):
    half = MLA_ROPE // 2
    ang = pos_ref[...].astype(F32) * freq_ref[...]
    lane = lax.broadcasted_iota(jnp.int32, ang.shape, 1)
    cos = jnp.cos(ang)
    sin = jnp.sin(ang)
    cos_ref[...] = jnp.where(lane < MLA_ROPE, cos, 1.0)
    sa_ref[...] = jnp.where(lane < half, -sin, 0.0)
    sb_ref[...] = jnp.where((lane >= half) & (lane < MLA_ROPE), sin, 0.0)


def _rope_tables(positions):
    t = positions.size
    half = MLA_ROPE // 2
    tm = min(2048, t)
    inv_freq = ROPE_THETA ** (-jnp.arange(half, dtype=F32) / half)
    freq = jnp.concatenate([inv_freq, inv_freq, jnp.zeros((LANES - MLA_ROPE,), F32)])
    spec = pl.BlockSpec((tm, LANES), lambda i: (i, 0))
    shp = jax.ShapeDtypeStruct((t, LANES), F32)
    return pl.pallas_call(
        _rope_table_kernel,
        grid=(t // tm,),
        in_specs=[pl.BlockSpec((tm, 1), lambda i: (i, 0)),
                  pl.BlockSpec((1, LANES), lambda i: (0, 0))],
        out_specs=[spec, spec, spec],
        out_shape=[shp, shp, shp],
        compiler_params=_cparams("parallel"),
        name="rope_tables",
    )(positions.reshape(t, 1), freq.reshape(1, LANES))


def _rope(x, cos, sa, sb):
    half = MLA_ROPE // 2
    return x * cos + pltpu.roll(x, LANES - half, axis=1) * sa + pltpu.roll(x, half, axis=1) * sb


def _mla_up_kernel(p_ref, cos_ref, sa_ref, sb_ref, qg_ref, wq_ref, kvg_ref, wk_ref, wvt_ref,
                   q_ref, k_ref, vt_ref, *, heads):
    p = p_ref[...]
    cos, sa, sb = cos_ref[...], sa_ref[...], sb_ref[...]

    def rms(x, g):
        return x * lax.rsqrt(jnp.mean(x * x, -1, keepdims=True) + RMS_EPS) * g

    q_lat = rms(p[:, :MLA_Q_LORA], qg_ref[...]).astype(BF16)
    kv_lat = rms(p[:, MLA_Q_LORA:MLA_Q_LORA + MLA_KV_LORA], kvg_ref[...]).astype(BF16)
    k_pe = p[:, MLA_Q_LORA + MLA_KV_LORA:]
    k_pe = jnp.concatenate([k_pe, jnp.zeros((k_pe.shape[0], LANES - MLA_ROPE), F32)], axis=1)
    k_pe = _rope(k_pe, cos, sa, sb).astype(BF16)

    q = _dot(q_lat, wq_ref[...]) * (MLA_SCALE * LOG2_E)
    k_nope = _dot(kv_lat, wk_ref[...])
    vt_ref[0] = _dot_nt(wvt_ref[...], kv_lat).astype(BF16)
    for h in range(heads):
        base = h * MLA_QK_PAD
        q_ref[:, base:base + MLA_NOPE] = q[:, base:base + MLA_NOPE].astype(BF16)
        q_ref[:, base + MLA_NOPE:base + MLA_QK_PAD] = _rope(
            q[:, base + MLA_NOPE:base + MLA_QK_PAD], cos, sa, sb).astype(BF16)
        k_ref[:, base:base + MLA_NOPE] = k_nope[:, h * MLA_NOPE:(h + 1) * MLA_NOPE].astype(BF16)
        k_ref[:, base + MLA_NOPE:base + MLA_QK_PAD] = k_pe


def _mla_up(p_mla, tables, q_norm_g, w_uq, kv_norm_g, w_ukv, *, seq):
    t, cols = p_mla.shape
    heads = w_uq.shape[1] // (MLA_NOPE + MLA_ROPE)
    tm = min(TOKEN_TILE, seq)
    wq = w_uq.reshape(MLA_Q_LORA, heads, MLA_NOPE + MLA_ROPE)
    wq = jnp.pad(wq, ((0, 0), (0, 0), (0, MLA_QK_PAD - MLA_NOPE - MLA_ROPE)))
    wq = wq.reshape(MLA_Q_LORA, heads * MLA_QK_PAD).astype(BF16)
    wkv = w_ukv.reshape(MLA_KV_LORA, heads, MLA_NOPE + MLA_V)
    wk = wkv[:, :, :MLA_NOPE].reshape(MLA_KV_LORA, heads * MLA_NOPE).astype(BF16)
    wvt = wkv[:, :, MLA_NOPE:].reshape(MLA_KV_LORA, heads * MLA_V).T.astype(BF16)
    tok = lambda n: pl.BlockSpec((tm, n), lambda i: (i, 0))
    full = lambda a: pl.BlockSpec(a.shape, lambda i: (0, 0))
    qg = q_norm_g.reshape(1, -1)
    kvg = kv_norm_g.reshape(1, -1)
    kern = functools.partial(_mla_up_kernel, heads=heads)
    return pl.pallas_call(
        kern,
        grid=(t // tm,),
        in_specs=[tok(cols), tok(LANES), tok(LANES), tok(LANES),
                  full(qg), full(wq), full(kvg), full(wk), full(wvt)],
        out_specs=[tok(heads * MLA_QK_PAD), tok(heads * MLA_QK_PAD),
                   pl.BlockSpec((1, heads * MLA_V, tm), lambda i: (i, 0, 0))],
        out_shape=[jax.ShapeDtypeStruct((t, heads * MLA_QK_PAD), BF16),
                   jax.ShapeDtypeStruct((t, heads * MLA_QK_PAD), BF16),
                   jax.ShapeDtypeStruct((t // tm, heads * MLA_V, tm), BF16)],
        compiler_params=_cparams("parallel"),
        name="mla_up",
    )(p_mla, *tables, qg, wq, kvg, wk, wvt)


def _attn_kernel(qi_ref, kj_ref, q_ref, k_ref, vt_ref, o_ref, m_ref, l_ref, acc_ref, *,
                 heads, tile):
    step_id = pl.program_id(1)
    i = qi_ref[step_id]
    j = kj_ref[step_id]

    @pl.when(j == 0)
    def _():
        m_ref[...] = jnp.full_like(m_ref, -jnp.inf)
        l_ref[...] = jnp.zeros_like(l_ref)
        acc_ref[...] = jnp.zeros_like(acc_ref)

    def qk(h):
        return slice(h * MLA_QK_PAD, (h + 1) * MLA_QK_PAD)

    def vo(h):
        return slice(h * MLA_V, (h + 1) * MLA_V)

    def step(masked, sub):
        hs = range(heads)
        qs = slice(sub * tile, (sub + 1) * tile)
        s = [_dot_nt(k_ref[:, qk(h)], q_ref[qs, qk(h)]) for h in hs]
        if masked:
            key = lax.broadcasted_iota(jnp.int32, s[0].shape, 0)
            query = lax.broadcasted_iota(jnp.int32, s[0].shape, 1)
            s = [jnp.where(key <= query, x, -jnp.inf) for x in s]
        m_prev = [m_ref[h, :, qs] for h in hs]
        m_new = [jnp.maximum(m_prev[h], jnp.max(s[h], 0, keepdims=True)) for h in hs]
        alpha = [jnp.exp2(m_prev[h] - m_new[h]) for h in hs]
        p = [jnp.exp2(s[h] - m_new[h]) for h in hs]
        pv = [_dot(vt_ref[0, vo(h), :], p[h].astype(BF16)) for h in hs]
        for h in hs:
            l_ref[h, :, qs] = alpha[h] * l_ref[h, :, qs] + jnp.sum(p[h], 0, keepdims=True)
            acc_ref[vo(h), qs] = alpha[h] * acc_ref[vo(h), qs] + pv[h]
            m_ref[h, :, qs] = m_new[h]

    for sub in range(ATTN_Q_TILES):
        diagonal = i * ATTN_Q_TILES + sub
        pl.when(j < diagonal)(functools.partial(step, False, sub))
        pl.when(j == diagonal)(functools.partial(step, True, sub))

    @pl.when(j == (i + 1) * ATTN_Q_TILES - 1)
    def _():
        for h in range(heads):
            o_ref[:, vo(h)] = (acc_ref[vo(h), :] / l_ref[h]).T.astype(o_ref.dtype)


def _attention(q, k, vt, *, batch, seq):
    t = q.shape[0]
    heads = vt.shape[1] // MLA_V
    tile = min(ATTN_TILE, seq)
    assert vt.shape[2] == tile
    tq = ATTN_Q_TILES * tile
    nq = seq // tq
    nk = seq // tile
    assert nq * tq == seq
    pairs = [(i, j) for i in range(nq) for j in range((i + 1) * ATTN_Q_TILES)]
    qi = jnp.asarray([p[0] for p in pairs], jnp.int32)
    kj = jnp.asarray([p[1] for p in pairs], jnp.int32)
    kern = functools.partial(_attn_kernel, heads=heads, tile=tile)
    grid_spec = pltpu.PrefetchScalarGridSpec(
        num_scalar_prefetch=2,
        grid=(batch, len(pairs)),
        in_specs=[pl.BlockSpec((tq, heads * MLA_QK_PAD), lambda b, s, qi, kj: (b * nq + qi[s], 0)),
                  pl.BlockSpec((tile, heads * MLA_QK_PAD),
                               lambda b, s, qi, kj: (b * nk + kj[s], 0)),
                  pl.BlockSpec((1, heads * MLA_V, tile),
                               lambda b, s, qi, kj: (b * nk + kj[s], 0, 0))],
        out_specs=pl.BlockSpec((tq, heads * MLA_V), lambda b, s, qi, kj: (b * nq + qi[s], 0)),
        scratch_shapes=[pltpu.VMEM((heads, 1, tq), F32), pltpu.VMEM((heads, 1, tq), F32),
                        pltpu.VMEM((heads * MLA_V, tq), F32)])
    return pl.pallas_call(
        kern,
        grid_spec=grid_spec,
        out_shape=jax.ShapeDtypeStruct((t, heads * MLA_V), BF16),
        compiler_params=_cparams("parallel", "arbitrary"),
        name="attention",
    )(qi, kj, q, k, vt)


def _out_proj_kernel(x_ref, mod_ref, y1_ref, y2_ref, y3_ref, w1_ref, w2_ref, w3_ref,
                     lng_ref, lnb_ref, o_ref, *, mod_base, alpha):
    y = (_dot(y1_ref[...], w1_ref[0]) + _dot(y2_ref[...], w2_ref[0])
         + _dot(y3_ref[...], w3_ref[0]))
    g = mod_ref[0, mod_base + 2:mod_base + 3, :]
    o_ref[...] = _layer_norm(alpha * x_ref[...] + (1.0 + g) * y, lng_ref[...], lnb_ref[...])


def _out_proj(x, mod, ys, w_out, ln_g, ln_b, *, layer, mod_base, alpha, seq):
    t, d = x.shape
    tm = min(TOKEN_TILE, seq)
    per_b = seq // tm
    tok = lambda n: pl.BlockSpec((tm, n), lambda i: (i, 0))
    full = lambda a: pl.BlockSpec(a.shape, lambda i: (0, 0))
    w_specs = []
    offset = 0
    for y in ys:
        width = y.shape[1]
        assert offset % width == 0
        w_specs.append(pl.BlockSpec((1, width, d), functools.partial(
            lambda i, blk: (layer, blk, 0), blk=offset // width)))
        offset += width
    assert offset == w_out.shape[1]
    kern = functools.partial(_out_proj_kernel, mod_base=mod_base, alpha=alpha)
    return pl.pallas_call(
        kern,
        grid=(t // tm,),
        in_specs=[tok(d), pl.BlockSpec((1, N_MOD, d), lambda i: (i // per_b, 0, 0)),
                  *[tok(y.shape[1]) for y in ys], *w_specs, full(ln_g), full(ln_b)],
        out_specs=tok(d),
        out_shape=jax.ShapeDtypeStruct((t, d), F32),
        compiler_params=_cparams("parallel"),
        name="out_proj",
    )(x, mod, *ys, w_out, w_out, w_out, ln_g, ln_b)


def kernel(x, c, positions, w_ada, b_ada, ln_g, ln_b, w_ffn1_in, w_ffn1_out, w_ffn2_in, w_ffn2_out, w_in, w_out, rw_mu, rw_w0, rw_w2, rw_a0, rw_a2, rw_g2, rw_k_k, rw_k_a, rw_r_k, rw_lnx_g, rw_lnx_b, mla_q_norm_g, mla_w_uq, mla_kv_norm_g, mla_w_ukv, conv_w, conv_b, conv_ln_g, conv_ln_b):
    batch, seq, d = x.shape
    depth = w_ada.shape[0]
    alpha = (2 * depth) ** 0.25
    rw_cols = rw_mu.shape[1]
    mla_cols = MLA_Q_LORA + MLA_KV_LORA + MLA_ROPE
    conv_cols = w_in.shape[2] - rw_cols - mla_cols

    mod_all = _ada(c, w_ada, b_ada).reshape(depth, batch, N_MOD, d)
    tables = _rope_tables(positions)
    xt = x.reshape(batch * seq, d)
    row = lambda a: a.reshape(1, -1)
    w_ffn1_in, w_ffn1_out, w_ffn2_in, w_ffn2_out, w_in, w_out = (
        w.astype(BF16) for w in (w_ffn1_in, w_ffn1_out, w_ffn2_in, w_ffn2_out, w_in, w_out))

    for l in range(depth):
        mod = mod_all[l]
        xt = _ffn(xt, mod, w_ffn1_in, w_ffn1_out, row(ln_g[l, 0]), row(ln_b[l, 0]),
                  layer=l, mod_base=0, alpha=alpha, seq=seq)

        p_rw, p_mla, y_conv = _in_proj(xt, mod, w_in, conv_w[l], conv_b[l], conv_ln_g[l],
                                       conv_ln_b[l], layer=l,
                                       cols=(rw_cols, mla_cols, conv_cols), mod_base=3, seq=seq)

        y_rw = _rwkv(p_rw, rw_mu[l], rw_w0[l], rw_w2[l], rw_a0[l], rw_a2[l], rw_g2[l],
                     rw_k_k[l], rw_k_a[l], rw_r_k[l], rw_lnx_g[l], rw_lnx_b[l],
                     batch=batch, seq=seq)
        q, k, v = _mla_up(p_mla, tables, mla_q_norm_g[l], mla_w_uq[l], mla_kv_norm_g[l],
                          mla_w_ukv[l], seq=seq)
        y_mla = _attention(q, k, v, batch=batch, seq=seq)

        xt = _out_proj(xt, mod, (y_rw, y_mla, y_conv), w_out, row(ln_g[l, 1]), row(ln_b[l, 1]),
                       layer=l, mod_base=3, alpha=alpha, seq=seq)

        xt = _ffn(xt, mod, w_ffn2_in, w_ffn2_out, row(ln_g[l, 2]), row(ln_b[l, 2]),
                  layer=l, mod_base=6, alpha=alpha, seq=seq)
    return xt.reshape(batch, seq, d)
```

```python
import functools

import jax
import jax.numpy as jnp
from jax import lax
from jax.experimental import pallas as pl
from jax.experimental.pallas import tpu as pltpu

F32 = jnp.float32
BF16 = jnp.bfloat16

LANES = 128
SUBLANES = 8
RW_HEAD_DIM = 64
RW_CHUNK = 64
RW_INV_BASE = 16
RW_OUT_DTYPE = jnp.bfloat16
RW_DECAY_LORA = 64
RW_AAA_LORA = 64
RW_GATE_LORA = 128
RW_GN_EPS = 64e-5
MLA_NOPE = 128
MLA_ROPE = 64
MLA_V = 128
MLA_Q_LORA = 384
MLA_KV_LORA = 256
MLA_QK_PAD = 256
MLA_SCALE = (MLA_NOPE + MLA_ROPE) ** -0.5
LOG2_E = 1.4426950408889634
ROPE_THETA = 10000.0
CONV_K = 31
CONV_HALO = 32
N_MOD = 9
FFN_RES = 0.5
LN_EPS = 1e-5
RMS_EPS = 1e-6
VMEM_LIMIT = 52 * 1024 * 1024
TOKEN_TILE = 512
ATTN_TILE = 512
ATTN_Q_TILES = 4
RW_BLOCK = 512
FFN_TILE = 512
FFN_TOKEN_TILE = 512


def _cparams(*sem):
    return pltpu.CompilerParams(dimension_semantics=sem, vmem_limit_bytes=VMEM_LIMIT)


def _dot(a, b):
    return jnp.dot(a, b, preferred_element_type=F32)


def _dot_nt(a, b):
    return lax.dot_general(a, b, (((1,), (1,)), ((), ())), preferred_element_type=F32)


def _rw_dot(a, b):
    return _dot(a.astype(BF16), b.astype(BF16))


def _split_bf16(x, terms):
    parts = []
    for _ in range(terms):
        hi = x.astype(BF16)
        parts.append(hi)
        x = x - hi.astype(F32)
    return parts


def _dot_split_rhs(exact_lhs, x, terms):
    return sum(_dot(exact_lhs, part) for part in _split_bf16(x, terms))


def _mm3(a, b):
    a_hi, a_lo = _split_bf16(a, 2)
    b_hi, b_lo = _split_bf16(b, 2)
    k, n = b.shape
    if k != LANES:
        return _dot(a_hi, b_hi) + _dot(a_lo, b_hi) + _dot(a_hi, b_lo)
    lhs = jnp.concatenate([a_hi, a_lo], axis=1)
    if n != LANES:
        return _dot(lhs, jnp.concatenate([b_hi, b_hi], axis=0)) + _dot(a_hi, b_lo)
    rhs = jnp.concatenate([jnp.concatenate([b_hi, b_lo], axis=1),
                           jnp.concatenate([b_hi, jnp.zeros_like(b_lo)], axis=1)], axis=0)
    out = _dot(lhs, rhs)
    return out[:, :n] + out[:, n:]


def _sigmoid(x):
    return 1.0 / (1.0 + jnp.exp(-x))


def _silu(x):
    return x * _sigmoid(x)


def _layer_norm(y, g, b):
    mean = jnp.mean(y, -1, keepdims=True)
    d = y - mean
    var = jnp.mean(d * d, -1, keepdims=True)
    return d * lax.rsqrt(var + LN_EPS) * g + b


def _ada_kernel(ct_ref, w_ref, b_ref, o_ref, *, batch):
    w = w_ref[0]
    rows = [jnp.sum(_silu(ct_ref[:, b:b + 1]) * w, axis=0, keepdims=True) for b in range(batch)]
    o_ref[0] = jnp.concatenate(rows, axis=0) + b_ref[0]


def _ada(c, w_ada, b_ada):
    depth, d, n = w_ada.shape
    batch = c.shape[0]
    tn = 1024
    kern = functools.partial(_ada_kernel, batch=batch)
    return pl.pallas_call(
        kern,
        grid=(depth, n // tn),
        in_specs=[pl.BlockSpec((d, batch), lambda l, j: (0, 0)),
                  pl.BlockSpec((1, d, tn), lambda l, j: (l, 0, j)),
                  pl.BlockSpec((1, 1, tn), lambda l, j: (l, 0, j))],
        out_specs=pl.BlockSpec((1, batch, tn), lambda l, j: (l, 0, j)),
        out_shape=jax.ShapeDtypeStruct((depth, batch, n), F32),
        compiler_params=_cparams("parallel", "parallel"),
        name="ada",
    )(c.T, w_ada, b_ada.reshape(depth, 1, n))


def _ffn_kernel(x_ref, mod_ref, wg_ref, wu_ref, wo_ref, lng_ref, lnb_ref, o_ref,
                h_ref, *, mod_base, alpha):
    j = pl.program_id(1)
    last = pl.num_programs(1) - 1

    def contribution(h):
        gate = _dot(h, wg_ref[0])
        up = _dot(h, wu_ref[0])
        act = (_silu(gate) * up).astype(BF16)
        return _dot(act, wo_ref[0])

    @pl.when(j == 0)
    def _():
        sh = mod_ref[0, mod_base:mod_base + 1, :]
        sc = mod_ref[0, mod_base + 1:mod_base + 2, :]
        h = (x_ref[...] * (1.0 + sc) + sh).astype(BF16)
        h_ref[...] = h
        o_ref[...] = contribution(h)

    @pl.when((j > 0) & (j < last))
    def _():
        o_ref[...] += contribution(h_ref[...])

    @pl.when(j == last)
    def _():
        acc = o_ref[...] + contribution(h_ref[...])
        g = mod_ref[0, mod_base + 2:mod_base + 3, :]
        y = alpha * x_ref[...] + (FFN_RES * (1.0 + g)) * acc
        o_ref[...] = _layer_norm(y, lng_ref[...], lnb_ref[...])


def _ffn(x, mod, w_in, w_out, ln_g, ln_b, *, layer, mod_base, alpha, seq):
    t, d = x.shape
    f = w_out.shape[1]
    tm = min(FFN_TOKEN_TILE, seq)
    tf = FFN_TILE
    nf = f // tf
    assert nf >= 2
    per_b = seq // tm
    kern = functools.partial(_ffn_kernel, mod_base=mod_base, alpha=alpha)
    return pl.pallas_call(
        kern,
        grid=(t // tm, nf),
        in_specs=[pl.BlockSpec((tm, d), lambda i, j: (i, 0)),
                  pl.BlockSpec((1, N_MOD, d), lambda i, j: (i // per_b, 0, 0)),
                  pl.BlockSpec((1, d, tf), lambda i, j: (layer, 0, j)),
                  pl.BlockSpec((1, d, tf), lambda i, j: (layer, 0, nf + j)),
                  pl.BlockSpec((1, tf, d), lambda i, j: (layer, j, 0)),
                  pl.BlockSpec((1, d), lambda i, j: (0, 0)),
                  pl.BlockSpec((1, d), lambda i, j: (0, 0))],
        out_specs=pl.BlockSpec((tm, d), lambda i, j: (i, 0)),
        out_shape=jax.ShapeDtypeStruct((t, d), F32),
        scratch_shapes=[pltpu.VMEM((tm, d), BF16)],
        compiler_params=_cparams("parallel", "arbitrary"),
        name="ffn",
    )(x, mod, w_in, w_in, w_out, ln_g, ln_b)


def _conv_branch(p_conv, u_ref, w_ref, b_ref, lng_ref, lnb_ref):
    ch = u_ref.shape[1]
    tm = p_conv.shape[0]
    u_ref[CONV_HALO:, :] = p_conv[:, :ch] * _sigmoid(p_conv[:, ch:])
    first = CONV_HALO - (CONV_K - 1)
    u = u_ref[...]
    rows = u.shape[0]
    acc = jnp.zeros((tm, ch), F32)
    for phase in range(SUBLANES):
        shifted = u if phase == 0 else pltpu.roll(u, rows - phase, axis=0)
        for base in range(0, CONV_HALO + 1, SUBLANES):
            tap = base + phase - first
            if 0 <= tap < CONV_K:
                acc = acc + shifted[base:base + tm, :] * w_ref[tap:tap + 1, :]
    halo = u_ref[tm:tm + CONV_HALO, :]
    u_ref[0:CONV_HALO, :] = halo
    return _silu(_layer_norm(acc + b_ref[...], lng_ref[...], lnb_ref[...]))


def _in_proj_kernel(x_ref, mod_ref, w_rw_ref, w_mla_ref, w_conv_ref,
                    cw_ref, cb_ref, clng_ref, clnb_ref,
                    rw_ref, mla_ref, yconv_ref, u_ref, *, mod_base, per_b):
    @pl.when(pl.program_id(0) % per_b == 0)
    def _():
        u_ref[0:CONV_HALO, :] = jnp.zeros((CONV_HALO, u_ref.shape[1]), F32)

    sh = mod_ref[0, mod_base:mod_base + 1, :]
    sc = mod_ref[0, mod_base + 1:mod_base + 2, :]
    h = (x_ref[...] * (1.0 + sc) + sh).astype(BF16)
    p_conv = _dot(h, w_conv_ref[0])
    rw_ref[...] = _dot(h, w_rw_ref[0])
    mla_ref[...] = _dot(h, w_mla_ref[0])
    yconv_ref[...] = _conv_branch(p_conv, u_ref, cw_ref, cb_ref, clng_ref,
                                  clnb_ref).astype(yconv_ref.dtype)


def _in_proj(x, mod, w_in, conv_w, conv_b, conv_ln_g, conv_ln_b, *, layer, cols, mod_base, seq):
    t, d = x.shape
    rw_cols, mla_cols, conv_cols = cols
    ch = conv_w.shape[1]
    assert conv_cols == 2 * ch
    tm = min(TOKEN_TILE, seq)
    per_b = seq // tm
    w_mla = w_in[layer:layer + 1, :, rw_cols:rw_cols + mla_cols]
    w_conv = w_in[layer:layer + 1, :, rw_cols + mla_cols:]
    tok = lambda n: pl.BlockSpec((tm, n), lambda i: (i, 0))
    full = lambda a: pl.BlockSpec(a.shape, lambda i: (0, 0))
    resident = lambda n, l: pl.BlockSpec((1, d, n), lambda i: (l, 0, 0),
                                         pipeline_mode=pl.Buffered(1))
    row = lambda a: a.reshape(1, -1)
    small = (conv_w, row(conv_b), row(conv_ln_g), row(conv_ln_b))
    kern = functools.partial(_in_proj_kernel, mod_base=mod_base, per_b=per_b)
    return pl.pallas_call(
        kern,
        grid=(t // tm,),
        in_specs=[tok(d), pl.BlockSpec((1, N_MOD, d), lambda i: (i // per_b, 0, 0)),
                  resident(rw_cols, layer), resident(mla_cols, 0), resident(conv_cols, 0),
                  *[full(a) for a in small]],
        out_specs=[tok(rw_cols), tok(mla_cols), tok(ch)],
        out_shape=[jax.ShapeDtypeStruct((t, rw_cols), F32),
                   jax.ShapeDtypeStruct((t, mla_cols), F32),
                   jax.ShapeDtypeStruct((t, ch), BF16)],
        scratch_shapes=[pltpu.VMEM((CONV_HALO + tm, ch), F32)],
        compiler_params=_cparams("arbitrary"),
        name="in_proj",
    )(x, mod, w_in, w_mla, w_conv, *small)


def _rwkv_kernel(r_ref, k_ref, v_ref, lo_ref, mur_ref, muk_ref, muv_ref, mulo_ref,
                 w0_ref, w2_ref, a0_ref, a2_ref, g2_ref, kk_ref, ka_ref, rk_ref,
                 lng_ref, lnb_ref, o_ref,
                 state_ref, pr_ref, pk_ref, pv_ref, plo_ref, obuf_ref, *, bb, tb):
    j = pl.program_id(1)
    L = RW_CHUNK
    nc = tb // L

    @pl.when(j == 0)
    def _():
        state_ref[...] = jnp.zeros_like(state_ref)
        pr_ref[...] = jnp.zeros_like(pr_ref)
        pk_ref[...] = jnp.zeros_like(pk_ref)
        pv_ref[...] = jnp.zeros_like(pv_ref)
        plo_ref[...] = jnp.zeros_like(plo_ref)

    lane_r = lax.broadcasted_iota(jnp.int32, (LANES, LANES), 0)
    lane_c = lax.broadcasted_iota(jnp.int32, (LANES, LANES), 1)

    def same_block(size):
        return (lane_r // size) == (lane_c // size)

    same_head = same_block(RW_HEAD_DIM).astype(BF16)
    stack_mask = ((lane_r // L) == (lane_c // RW_HEAD_DIM)).astype(F32)
    strict_lower = lane_r > lane_c
    lower = lane_r >= lane_c
    eye = (lane_r == lane_c).astype(F32)
    tri = (lax.broadcasted_iota(jnp.int32, (L, L), 0)
           >= lax.broadcasted_iota(jnp.int32, (L, L), 1)).astype(BF16)
    zeros = jnp.zeros((LANES, LANES), F32)

    def head_sum(x):
        return _dot(x.astype(BF16), same_head)

    def stack(x):
        return jnp.concatenate([x, x], axis=0) * stack_mask

    def prologue(b):
        def shift_mix(p_ref, prev_ref, mu_ref):
            p = p_ref[b]
            row = lax.broadcasted_iota(jnp.int32, p.shape, 0)
            prev = jnp.where(row == 0, prev_ref[b, 0:1, :], pltpu.roll(p, 1, axis=0))
            prev_ref[b, 0:1, :] = p[tb - 1:tb, :]
            return p + (prev - p) * mu_ref[...]

        r = shift_mix(r_ref, pr_ref, mur_ref)
        k = shift_mix(k_ref, pk_ref, muk_ref)
        v = shift_mix(v_ref, pv_ref, muv_ref)
        lo = shift_mix(lo_ref, plo_ref, mulo_ref)
        w_lo = lo[:, :RW_DECAY_LORA]
        a_lo = lo[:, RW_DECAY_LORA:RW_DECAY_LORA + RW_AAA_LORA]
        g_lo = lo[:, RW_DECAY_LORA + RW_AAA_LORA:]
        z = w0_ref[...] + _rw_dot(jnp.tanh(w_lo), w2_ref[...])
        softplus_neg_z = jnp.maximum(-z, 0.0) + jnp.log(1.0 + jnp.exp(-jnp.abs(z)))
        log_decay = -jnp.exp(-softplus_neg_z - 0.5)
        a = _sigmoid(a0_ref[...] + _rw_dot(a_lo, a2_ref[...]))
        gate = _rw_dot(_sigmoid(g_lo), g2_ref[...])
        kk = k * kk_ref[...]
        kk = kk / jnp.maximum(jnp.sqrt(head_sum(kk * kk)), 1e-12)
        k = k * (1.0 + (a - 1.0) * ka_ref[...])
        return r, k, v, kk, a, gate, log_decay

    seqs = [prologue(b) for b in range(bb)]
    items = [(b, c) for c in range(nc) for b in range(bb)]
    every = range(len(items))

    def chunk_operands(b, c):
        r, k, v, kk, a, _, log_decay = seqs[b]
        rows = slice(c * L, (c + 1) * L)
        lw = log_decay[rows]
        cum = _dot_split_rhs(tri, lw, 3)
        w_cum = jnp.exp(cum)
        w_inv = jnp.exp(-cum)
        w_prev = jnp.exp(cum - lw)
        w_last = w_cum[L - 1:L, :]
        kk_c = kk[rows]
        return (stack(-kk_c * w_prev), stack(kk_c * a[rows] * w_inv), stack(k[rows] * w_inv),
                stack(r[rows] * w_cum), stack(v[rows]), w_last)

    ops = [chunk_operands(b, c) for b, c in items]
    a2_ = [o[0] for o in ops]
    b2_ = [o[1] for o in ops]
    k2_ = [o[2] for o in ops]
    r2_ = [o[3] for o in ops]
    v2_ = [o[4] for o in ops]
    w_last = [o[5] for o in ops]

    scores = [_dot_nt(jnp.concatenate([a2_[i], r2_[i]], axis=0).astype(BF16),
                      jnp.concatenate([b2_[i], k2_[i]], axis=0).astype(BF16)) for i in every]
    a_ab = [jnp.where(strict_lower, s[:LANES, :LANES], 0.0) for s in scores]
    a_ak = [jnp.where(strict_lower, s[:LANES, LANES:], 0.0) for s in scores]
    a_rb = [jnp.where(lower, s[LANES:, :LANES], 0.0) for s in scores]
    a_rk = [jnp.where(lower, s[LANES:, LANES:], 0.0) for s in scores]

    diag_blocks = same_block(RW_INV_BASE)
    p = [jnp.where(diag_blocks, x, 0.0) for x in a_ab]
    t = [eye + x for x in p]
    n = 2
    while n < RW_INV_BASE:
        p = [_mm3(x, x) for x in p]
        t = [t[i] + _mm3(t[i], p[i]) for i in every]
        n *= 2
    size = 2 * RW_INV_BASE
    while size <= L:
        off_diag = same_block(size) & ~same_block(size // 2)
        et = [_mm3(jnp.where(off_diag, a_ab[i], 0.0), t[i]) for i in every]
        t = [t[i] + _mm3(t[i], et[i]) for i in every]
        size *= 2

    akv = [_rw_dot(a_ak[i], v2_[i]) for i in every]
    x = [_mm3(t[i], jnp.concatenate([a2_[i], akv[i]], axis=1)) for i in every]
    big = []
    for i in every:
        lhs = jnp.concatenate(
            [jnp.concatenate([a_rb[i], a_rk[i]], axis=1),
             jnp.concatenate([(b2_[i] * w_last[i]).T, (k2_[i] * w_last[i]).T], axis=1)], axis=0)
        rhs = jnp.concatenate([x[i], jnp.concatenate([zeros, v2_[i]], axis=1)], axis=0)
        big.append(_mm3(lhs, rhs))

    states = [state_ref[b] for b in range(bb)]
    for i, (b, c) in enumerate(items):
        r_hat = r2_[i] + big[i][:LANES, :LANES]
        st = _rw_dot(jnp.concatenate([r_hat, big[i][LANES:, :LANES]], axis=0), states[b])
        o2 = st[:LANES] + big[i][:LANES, LANES:]
        obuf_ref[b, c * L:(c + 1) * L, :] = o2[:L] + o2[L:]
        w_rows = jnp.broadcast_to(w_last[i], (LANES, LANES)).T
        states[b] = states[b] * w_rows + st[LANES:] + big[i][LANES:, LANES:]
    for b in range(bb):
        state_ref[b] = states[b]

    inv_n = 1.0 / RW_HEAD_DIM
    for b in range(bb):
        r, k, v, _, _, gate, _ = seqs[b]
        o = obuf_ref[b]
        mean = head_sum(o) * inv_n
        d = o - mean
        var = head_sum(d * d) * inv_n
        on = d * lax.rsqrt(var + RW_GN_EPS) * lng_ref[...] + lnb_ref[...]
        bonus = head_sum(r * k * rk_ref[...]) * v
        o_ref[b] = ((on + bonus) * gate).astype(o_ref.dtype)


def _rwkv(p_rw, mu, w0, w2, a0, a2, g2, k_k, k_a, r_k, lnx_g, lnx_b, *, batch, seq):
    width = w0.shape[-1]
    pairs = width // LANES
    tb = min(RW_BLOCK, seq)
    lo_w = RW_DECAY_LORA + RW_AAA_LORA + RW_GATE_LORA
    lo_blk = 3 * width // lo_w
    p3 = p_rw.reshape(batch, seq, p_rw.shape[-1])

    def col(off):
        return lambda h, j: (0, j, off + h)

    def vec(off):
        return lambda h, j: (0, off + h)

    row = lambda a: a.reshape(1, -1)
    kern = functools.partial(_rwkv_kernel, bb=batch, tb=tb)
    out = pl.pallas_call(
        kern,
        grid=(pairs, seq // tb),
        in_specs=[pl.BlockSpec((batch, tb, LANES), col(0)),
                  pl.BlockSpec((batch, tb, LANES), col(pairs)),
                  pl.BlockSpec((batch, tb, LANES), col(2 * pairs)),
                  pl.BlockSpec((batch, tb, lo_w), lambda h, j: (0, j, lo_blk)),
                  pl.BlockSpec((1, LANES), vec(0)),
                  pl.BlockSpec((1, LANES), vec(pairs)),
                  pl.BlockSpec((1, LANES), vec(2 * pairs)),
                  pl.BlockSpec((1, lo_w), lambda h, j: (0, lo_blk)),
                  pl.BlockSpec((1, LANES), vec(0)),
                  pl.BlockSpec((RW_DECAY_LORA, LANES), vec(0)),
                  pl.BlockSpec((1, LANES), vec(0)),
                  pl.BlockSpec((RW_AAA_LORA, LANES), vec(0)),
                  pl.BlockSpec((RW_GATE_LORA, LANES), vec(0)),
                  pl.BlockSpec((1, LANES), vec(0)),
                  pl.BlockSpec((1, LANES), vec(0)),
                  pl.BlockSpec((1, LANES), vec(0)),
                  pl.BlockSpec((1, LANES), vec(0)),
                  pl.BlockSpec((1, LANES), vec(0))],
        out_specs=pl.BlockSpec((batch, tb, LANES), lambda h, j: (0, j, h)),
        out_shape=jax.ShapeDtypeStruct((batch, seq, width), RW_OUT_DTYPE),
        scratch_shapes=[pltpu.VMEM((batch, LANES, LANES), F32),
                        pltpu.VMEM((batch, SUBLANES, LANES), F32),
                        pltpu.VMEM((batch, SUBLANES, LANES), F32),
                        pltpu.VMEM((batch, SUBLANES, LANES), F32),
                        pltpu.VMEM((batch, SUBLANES, lo_w), F32),
                        pltpu.VMEM((batch, tb, LANES), F32)],
        compiler_params=_cparams("parallel", "arbitrary"),
        name="rwkv",
    )(p3, p3, p3, p3, row(mu), row(mu), row(mu), row(mu), row(w0), w2, row(a0), a2, g2,
      row(k_k), row(k_a), row(r_k), row(lnx_g), row(lnx_b))
    return out.reshape(batch * seq, width)


def _rope_table_kernel(pos_ref, freq_ref, cos_ref, sa_ref, sb_ref):
    half = MLA_ROPE // 2
    ang = pos_ref[...].astype(F32) * freq_ref[...]
    lane = lax.broadcasted_iota(jnp.int32, ang.shape, 1)
    cos = jnp.cos(ang)
    sin = jnp.sin(ang)
    cos_ref[...] = jnp.where(lane < MLA_ROPE, cos, 1.0)
    sa_ref[...] = jnp.where(lane < half, -sin, 0.0)
    sb_ref[...] = jnp.where((lane >= half) & (lane < MLA_ROPE), sin, 0.0)


def _rope_tables(positions):
    t = positions.size
    half = MLA_ROPE // 2
    tm = min(2048, t)
    inv_freq = ROPE_THETA ** (-jnp.arange(half, dtype=F32) / half)
    freq = jnp.concatenate([inv_freq, inv_freq, jnp.zeros((LANES - MLA_ROPE,), F32)])
    spec = pl.BlockSpec((tm, LANES), lambda i: (i, 0))
    shp = jax.ShapeDtypeStruct((t, LANES), F32)
    return pl.pallas_call(
        _rope_table_kernel,
        grid=(t // tm,),
        in_specs=[pl.BlockSpec((tm, 1), lambda i: (i, 0)),
                  pl.BlockSpec((1, LANES), lambda i: (0, 0))],
        out_specs=[spec, spec, spec],
        out_shape=[shp, shp, shp],
        compiler_params=_cparams("parallel"),
        name="rope_tables",
    )(positions.reshape(t, 1), freq.reshape(1, LANES))


def _rope(x, cos, sa, sb):
    half = MLA_ROPE // 2
    return x * cos + pltpu.roll(x, LANES - half, axis=1) * sa + pltpu.roll(x, half, axis=1) * sb


def _mla_up_kernel(p_ref, cos_ref, sa_ref, sb_ref, qg_ref, wq_ref, kvg_ref, wk_ref, wvt_ref,
                   q_ref, k_ref, vt_ref, *, heads):
    p = p_ref[...]
    cos, sa, sb = cos_ref[...], sa_ref[...], sb_ref[...]

    def rms(x, g):
        return x * lax.rsqrt(jnp.mean(x * x, -1, keepdims=True) + RMS_EPS) * g

    q_lat = rms(p[:, :MLA_Q_LORA], qg_ref[...]).astype(BF16)
    kv_lat = rms(p[:, MLA_Q_LORA:MLA_Q_LORA + MLA_KV_LORA], kvg_ref[...]).astype(BF16)
    k_pe = p[:, MLA_Q_LORA + MLA_KV_LORA:]
    k_pe = jnp.concatenate([k_pe, jnp.zeros((k_pe.shape[0], LANES - MLA_ROPE), F32)], axis=1)
    k_pe = _rope(k_pe, cos, sa, sb).astype(BF16)

    q = _dot(q_lat, wq_ref[...]) * (MLA_SCALE * LOG2_E)
    k_nope = _dot(kv_lat, wk_ref[...])
    vt_ref[0] = _dot_nt(wvt_ref[...], kv_lat).astype(BF16)
    for h in range(heads):
        base = h * MLA_QK_PAD
        q_ref[:, base:base + MLA_NOPE] = q[:, base:base + MLA_NOPE].astype(BF16)
        q_ref[:, base + MLA_NOPE:base + MLA_QK_PAD] = _rope(
            q[:, base + MLA_NOPE:base + MLA_QK_PAD], cos, sa, sb).astype(BF16)
        k_ref[:, base:base + MLA_NOPE] = k_nope[:, h * MLA_NOPE:(h + 1) * MLA_NOPE].astype(BF16)
        k_ref[:, base + MLA_NOPE:base + MLA_QK_PAD] = k_pe


def _mla_up(p_mla, tables, q_norm_g, w_uq, kv_norm_g, w_ukv, *, seq):
    t, cols = p_mla.shape
    heads = w_uq.shape[1] // (MLA_NOPE + MLA_ROPE)
    tm = min(TOKEN_TILE, seq)
    wq = w_uq.reshape(MLA_Q_LORA, heads, MLA_NOPE + MLA_ROPE)
    wq = jnp.pad(wq, ((0, 0), (0, 0), (0, MLA_QK_PAD - MLA_NOPE - MLA_ROPE)))
    wq = wq.reshape(MLA_Q_LORA, heads * MLA_QK_PAD).astype(BF16)
    wkv = w_ukv.reshape(MLA_KV_LORA, heads, MLA_NOPE + MLA_V)
    wk = wkv[:, :, :MLA_NOPE].reshape(MLA_KV_LORA, heads * MLA_NOPE).astype(BF16)
    wvt = wkv[:, :, MLA_NOPE:].reshape(MLA_KV_LORA, heads * MLA_V).T.astype(BF16)
    tok = lambda n: pl.BlockSpec((tm, n), lambda i: (i, 0))
    full = lambda a: pl.BlockSpec(a.shape, lambda i: (0, 0))
    qg = q_norm_g.reshape(1, -1)
    kvg = kv_norm_g.reshape(1, -1)
    kern = functools.partial(_mla_up_kernel, heads=heads)
    return pl.pallas_call(
        kern,
        grid=(t // tm,),
        in_specs=[tok(cols), tok(LANES), tok(LANES), tok(LANES),
                  full(qg), full(wq), full(kvg), full(wk), full(wvt)],
        out_specs=[tok(heads * MLA_QK_PAD), tok(heads * MLA_QK_PAD),
                   pl.BlockSpec((1, heads * MLA_V, tm), lambda i: (i, 0, 0))],
        out_shape=[jax.ShapeDtypeStruct((t, heads * MLA_QK_PAD), BF16),
                   jax.ShapeDtypeStruct((t, heads * MLA_QK_PAD), BF16),
                   jax.ShapeDtypeStruct((t // tm, heads * MLA_V, tm), BF16)],
        compiler_params=_cparams("parallel"),
        name="mla_up",
    )(p_mla, *tables, qg, wq, kvg, wk, wvt)


def _attn_kernel(qi_ref, kj_ref, q_ref, k_ref, vt_ref, o_ref, m_ref, l_ref, acc_ref, *,
                 heads, tile):
    step_id = pl.program_id(1)
    i = qi_ref[step_id]
    j = kj_ref[step_id]

    @pl.when(j == 0)
    def _():
        m_ref[...] = jnp.full_like(m_ref, -jnp.inf)
        l_ref[...] = jnp.zeros_like(l_ref)
        acc_ref[...] = jnp.zeros_like(acc_ref)

    def qk(h):
        return slice(h * MLA_QK_PAD, (h + 1) * MLA_QK_PAD)

    def vo(h):
        return slice(h * MLA_V, (h + 1) * MLA_V)

    def step(masked, sub):
        hs = range(heads)
        qs = slice(sub * tile, (sub + 1) * tile)
        s = [_dot_nt(k_ref[:, qk(h)], q_ref[qs, qk(h)]) for h in hs]
        if masked:
            key = lax.broadcasted_iota(jnp.int32, s[0].shape, 0)
            query = lax.broadcasted_iota(jnp.int32, s[0].shape, 1)
            s = [jnp.where(key <= query, x, -jnp.inf) for x in s]
        m_prev = [m_ref[h, :, qs] for h in hs]
        m_new = [jnp.maximum(m_prev[h], jnp.max(s[h], 0, keepdims=True)) for h in hs]
        alpha = [jnp.exp2(m_prev[h] - m_new[h]) for h in hs]
        p = [jnp.exp2(s[h] - m_new[h]) for h in hs]
        pv = [_dot(vt_ref[0, vo(h), :], p[h].astype(BF16)) for h in hs]
        for h in hs:
            l_ref[h, :, qs] = alpha[h] * l_ref[h, :, qs] + jnp.sum(p[h], 0, keepdims=True)
            acc_ref[vo(h), qs] = alpha[h] * acc_ref[vo(h), qs] + pv[h]
            m_ref[h, :, qs] = m_new[h]

    for sub in range(ATTN_Q_TILES):
        diagonal = i * ATTN_Q_TILES + sub
        pl.when(j < diagonal)(functools.partial(step, False, sub))
        pl.when(j == diagonal)(functools.partial(step, True, sub))

    @pl.when(j == (i + 1) * ATTN_Q_TILES - 1)
    def _():
        for h in range(heads):
            o_ref[:, vo(h)] = (acc_ref[vo(h), :] / l_ref[h]).T.astype(o_ref.dtype)


def _attention(q, k, vt, *, batch, seq):
    t = q.shape[0]
    heads = vt.shape[1] // MLA_V
    tile = min(ATTN_TILE, seq)
    assert vt.shape[2] == tile
    tq = ATTN_Q_TILES * tile
    nq = seq // tq
    nk = seq // tile
    assert nq * tq == seq
    pairs = [(i, j) for i in range(nq) for j in range((i + 1) * ATTN_Q_TILES)]
    qi = jnp.asarray([p[0] for p in pairs], jnp.int32)
    kj = jnp.asarray([p[1] for p in pairs], jnp.int32)
    kern = functools.partial(_attn_kernel, heads=heads, tile=tile)
    grid_spec = pltpu.PrefetchScalarGridSpec(
        num_scalar_prefetch=2,
        grid=(batch, len(pairs)),
        in_specs=[pl.BlockSpec((tq, heads * MLA_QK_PAD), lambda b, s, qi, kj: (b * nq + qi[s], 0)),
                  pl.BlockSpec((tile, heads * MLA_QK_PAD),
                               lambda b, s, qi, kj: (b * nk + kj[s], 0)),
                  pl.BlockSpec((1, heads * MLA_V, tile),
                               lambda b, s, qi, kj: (b * nk + kj[s], 0, 0))],
        out_specs=pl.BlockSpec((tq, heads * MLA_V), lambda b, s, qi, kj: (b * nq + qi[s], 0)),
        scratch_shapes=[pltpu.VMEM((heads, 1, tq), F32), pltpu.VMEM((heads, 1, tq), F32),
                        pltpu.VMEM((heads * MLA_V, tq), F32)])
    return pl.pallas_call(
        kern,
        grid_spec=grid_spec,
        out_shape=jax.ShapeDtypeStruct((t, heads * MLA_V), BF16),
        compiler_params=_cparams("parallel", "arbitrary"),
        name="attention",
    )(qi, kj, q, k, vt)


def _out_proj_kernel(x_ref, mod_ref, y1_ref, y2_ref, y3_ref, w1_ref, w2_ref, w3_ref,
                     lng_ref, lnb_ref, o_ref, *, mod_base, alpha):
    y = (_dot(y1_ref[...], w1_ref[0]) + _dot(y2_ref[...], w2_ref[0])
         + _dot(y3_ref[...], w3_ref[0]))
    g = mod_ref[0, mod_base + 2:mod_base + 3, :]
    o_ref[...] = _layer_norm(alpha * x_ref[...] + (1.0 + g) * y, lng_ref[...], lnb_ref[...])


def _out_proj(x, mod, ys, w_out, ln_g, ln_b, *, layer, mod_base, alpha, seq):
    t, d = x.shape
    tm = min(TOKEN_TILE, seq)
    per_b = seq // tm
    tok = lambda n: pl.BlockSpec((tm, n), lambda i: (i, 0))
    full = lambda a: pl.BlockSpec(a.shape, lambda i: (0, 0))
    w_specs = []
    offset = 0
    for y in ys:
        width = y.shape[1]
        assert offset % width == 0
        w_specs.append(pl.BlockSpec((1, width, d), functools.partial(
            lambda i, blk: (layer, blk, 0), blk=offset // width)))
        offset += width
    assert offset == w_out.shape[1]
    kern = functools.partial(_out_proj_kernel, mod_base=mod_base, alpha=alpha)
    return pl.pallas_call(
        kern,
        grid=(t // tm,),
        in_specs=[tok(d), pl.BlockSpec((1, N_MOD, d), lambda i: (i // per_b, 0, 0)),
                  *[tok(y.shape[1]) for y in ys], *w_specs, full(ln_g), full(ln_b)],
        out_specs=tok(d),
        out_shape=jax.ShapeDtypeStruct((t, d), F32),
        compiler_params=_cparams("parallel"),
        name="out_proj",
    )(x, mod, *ys, w_out, w_out, w_out, ln_g, ln_b)


def kernel(x, c, positions, w_ada, b_ada, ln_g, ln_b, w_ffn1_in, w_ffn1_out, w_ffn2_in, w_ffn2_out, w_in, w_out, rw_mu, rw_w0, rw_w2, rw_a0, rw_a2, rw_g2, rw_k_k, rw_k_a, rw_r_k, rw_lnx_g, rw_lnx_b, mla_q_norm_g, mla_w_uq, mla_kv_norm_g, mla_w_ukv, conv_w, conv_b, conv_ln_g, conv_ln_b):
    batch, seq, d = x.shape
    depth = w_ada.shape[0]
    alpha = (2 * depth) ** 0.25
    rw_cols = rw_mu.shape[1]
    mla_cols = MLA_Q_LORA + MLA_KV_LORA + MLA_ROPE
    conv_cols = w_in.shape[2] - rw_cols - mla_cols

    mod_all = _ada(c, w_ada, b_ada).reshape(depth, batch, N_MOD, d)
    tables = _rope_tables(positions)
    xt = x.reshape(batch * seq, d)
    row = lambda a: a.reshape(1, -1)
    w_ffn1_in, w_ffn1_out, w_ffn2_in, w_ffn2_out, w_in, w_out = (
        w.astype(BF16) for w in (w_ffn1_in, w_ffn1_out, w_ffn2_in, w_ffn2_out, w_in, w_out))

    for l in range(depth):
        mod = mod_all[l]
        xt = _ffn(xt, mod, w_ffn1_in, w_ffn1_out, row(ln_g[l, 0]), row(ln_b[l, 0]),
                  layer=l, mod_base=0, alpha=alpha, seq=seq)

        p_rw, p_mla, y_conv = _in_proj(xt, mod, w_in, conv_w[l], conv_b[l], conv_ln_g[l],
                                       conv_ln_b[l], layer=l,
                                       cols=(rw_cols, mla_cols, conv_cols), mod_base=3, seq=seq)

        y_rw = _rwkv(p_rw, rw_mu[l], rw_w0[l], rw_w2[l], rw_a0[l], rw_a2[l], rw_g2[l],
                     rw_k_k[l], rw_k_a[l], rw_r_k[l], rw_lnx_g[l], rw_lnx_b[l],
                     batch=batch, seq=seq)
        q, k, v = _mla_up(p_mla, tables, mla_q_norm_g[l], mla_w_uq[l], mla_kv_norm_g[l],
                          mla_w_ukv[l], seq=seq)
        y_mla = _attention(q, k, v, batch=batch, seq=seq)

        xt = _out_proj(xt, mod, (y_rw, y_mla, y_conv), w_out, row(ln_g[l, 1]), row(ln_b[l, 1]),
                       layer=l, mod_base=3, alpha=alpha, seq=seq)

        xt = _ffn(xt, mod, w_ffn2_in, w_ffn2_out, row(ln_g[l, 2]), row(ln_b[l, 2]),
                  layer=l, mod_base=6, alpha=alpha, seq=seq)
    return xt.reshape(batch, seq, d)
```

```python
import functools

import jax
import jax.numpy as jnp
from jax import lax
from jax.experimental import pallas as pl
from jax.experimental.pallas import tpu as pltpu

F32 = jnp.float32
BF16 = jnp.bfloat16

LANES = 128
SUBLANES = 8
RW_HEAD_DIM = 64
RW_CHUNK = 64
RW_INV_BASE = 16
RW_OUT_DTYPE = jnp.bfloat16
RW_DECAY_LORA = 64
RW_AAA_LORA = 64
RW_GATE_LORA = 128
RW_GN_EPS = 64e-5
MLA_NOPE = 128
MLA_ROPE = 64
MLA_V = 128
MLA_Q_LORA = 384
MLA_KV_LORA = 256
MLA_QK_PAD = 256
MLA_SCALE = (MLA_NOPE + MLA_ROPE) ** -0.5
LOG2_E = 1.4426950408889634
ROPE_THETA = 10000.0
CONV_K = 31
CONV_HALO = 32
N_MOD = 9
FFN_RES = 0.5
LN_EPS = 1e-5
RMS_EPS = 1e-6
VMEM_LIMIT = 52 * 1024 * 1024
TOKEN_TILE = 512
ATTN_TILE = 512
ATTN_Q_TILES = 2
RW_BLOCK = 1024
FFN_TILE = 512
FFN_TOKEN_TILE = 512


def _cparams(*sem):
    return pltpu.CompilerParams(dimension_semantics=sem, vmem_limit_bytes=VMEM_LIMIT)


def _dot(a, b):
    return jnp.dot(a, b, preferred_element_type=F32)


def _dot_nt(a, b):
    return lax.dot_general(a, b, (((1,), (1,)), ((), ())), preferred_element_type=F32)


def _rw_dot(a, b):
    return _dot(a.astype(BF16), b.astype(BF16))


def _split_bf16(x, terms):
    parts = []
    for _ in range(terms):
        hi = x.astype(BF16)
        parts.append(hi)
        x = x - hi.astype(F32)
    return parts


def _dot_split_rhs(exact_lhs, x, terms):
    return sum(_dot(exact_lhs, part) for part in _split_bf16(x, terms))


def _mm3(a, b):
    a_hi, a_lo = _split_bf16(a, 2)
    b_hi, b_lo = _split_bf16(b, 2)
    k, n = b.shape
    if k != LANES:
        return _dot(a_hi, b_hi) + _dot(a_lo, b_hi) + _dot(a_hi, b_lo)
    lhs = jnp.concatenate([a_hi, a_lo], axis=1)
    if n != LANES:
        return _dot(lhs, jnp.concatenate([b_hi, b_hi], axis=0)) + _dot(a_hi, b_lo)
    rhs = jnp.concatenate([jnp.concatenate([b_hi, b_lo], axis=1),
                           jnp.concatenate([b_hi, jnp.zeros_like(b_lo)], axis=1)], axis=0)
    out = _dot(lhs, rhs)
    return out[:, :n] + out[:, n:]


def _sigmoid(x):
    return 1.0 / (1.0 + jnp.exp(-x))


def _silu(x):
    return x * _sigmoid(x)


def _layer_norm(y, g, b):
    mean = jnp.mean(y, -1, keepdims=True)
    d = y - mean
    var = jnp.mean(d * d, -1, keepdims=True)
    return d * lax.rsqrt(var + LN_EPS) * g + b


def _ada_kernel(ct_ref, w_ref, b_ref, o_ref, *, batch):
    w = w_ref[0]
    rows = [jnp.sum(_silu(ct_ref[:, b:b + 1]) * w, axis=0, keepdims=True) for b in range(batch)]
    o_ref[0] = jnp.concatenate(rows, axis=0) + b_ref[0]


def _ada(c, w_ada, b_ada):
    depth, d, n = w_ada.shape
    batch = c.shape[0]
    tn = 1024
    kern = functools.partial(_ada_kernel, batch=batch)
    return pl.pallas_call(
        kern,
        grid=(depth, n // tn),
        in_specs=[pl.BlockSpec((d, batch), lambda l, j: (0, 0)),
                  pl.BlockSpec((1, d, tn), lambda l, j: (l, 0, j)),
                  pl.BlockSpec((1, 1, tn), lambda l, j: (l, 0, j))],
        out_specs=pl.BlockSpec((1, batch, tn), lambda l, j: (l, 0, j)),
        out_shape=jax.ShapeDtypeStruct((depth, batch, n), F32),
        compiler_params=_cparams("parallel", "parallel"),
        name="ada",
    )(c.T, w_ada, b_ada.reshape(depth, 1, n))


def _ffn_kernel(x_ref, mod_ref, wg_ref, wu_ref, wo_ref, lng_ref, lnb_ref, o_ref,
                h_ref, *, mod_base, alpha):
    j = pl.program_id(1)
    last = pl.num_programs(1) - 1

    def contribution(h):
        gate = _dot(h, wg_ref[0])
        up = _dot(h, wu_ref[0])
        act = (_silu(gate) * up).astype(BF16)
        return _dot(act, wo_ref[0])

    @pl.when(j == 0)
    def _():
        sh = mod_ref[0, mod_base:mod_base + 1, :]
        sc = mod_ref[0, mod_base + 1:mod_base + 2, :]
        h = (x_ref[...] * (1.0 + sc) + sh).astype(BF16)
        h_ref[...] = h
        o_ref[...] = contribution(h)

    @pl.when((j > 0) & (j < last))
    def _():
        o_ref[...] += contribution(h_ref[...])

    @pl.when(j == last)
    def _():
        acc = o_ref[...] + contribution(h_ref[...])
        g = mod_ref[0, mod_base + 2:mod_base + 3, :]
        y = alpha * x_ref[...] + (FFN_RES * (1.0 + g)) * acc
        o_ref[...] = _layer_norm(y, lng_ref[...], lnb_ref[...])


def _ffn(x, mod, w_in, w_out, ln_g, ln_b, *, layer, mod_base, alpha, seq):
    t, d = x.shape
    f = w_out.shape[1]
    tm = min(FFN_TOKEN_TILE, seq)
    tf = FFN_TILE
    nf = f // tf
    assert nf >= 2
    per_b = seq // tm
    kern = functools.partial(_ffn_kernel, mod_base=mod_base, alpha=alpha)
    return pl.pallas_call(
        kern,
        grid=(t // tm, nf),
        in_specs=[pl.BlockSpec((tm, d), lambda i, j: (i, 0)),
                  pl.BlockSpec((1, N_MOD, d), lambda i, j: (i // per_b, 0, 0)),
                  pl.BlockSpec((1, d, tf), lambda i, j: (layer, 0, j)),
                  pl.BlockSpec((1, d, tf), lambda i, j: (layer, 0, nf + j)),
                  pl.BlockSpec((1, tf, d), lambda i, j: (layer, j, 0)),
                  pl.BlockSpec((1, d), lambda i, j: (0, 0)),
                  pl.BlockSpec((1, d), lambda i, j: (0, 0))],
        out_specs=pl.BlockSpec((tm, d), lambda i, j: (i, 0)),
        out_shape=jax.ShapeDtypeStruct((t, d), F32),
        scratch_shapes=[pltpu.VMEM((tm, d), BF16)],
        compiler_params=_cparams("parallel", "arbitrary"),
        name="ffn",
    )(x, mod, w_in, w_in, w_out, ln_g, ln_b)


def _conv_branch(p_conv, u_ref, w_ref, b_ref, lng_ref, lnb_ref):
    ch = u_ref.shape[1]
    tm = p_conv.shape[0]
    u_ref[CONV_HALO:, :] = p_conv[:, :ch] * _sigmoid(p_conv[:, ch:])
    first = CONV_HALO - (CONV_K - 1)
    u = u_ref[...]
    rows = u.shape[0]
    acc = jnp.zeros((tm, ch), F32)
    for phase in range(SUBLANES):
        shifted = u if phase == 0 else pltpu.roll(u, rows - phase, axis=0)
        for base in range(0, CONV_HALO + 1, SUBLANES):
            tap = base + phase - first
            if 0 <= tap < CONV_K:
                acc = acc + shifted[base:base + tm, :] * w_ref[tap:tap + 1, :]
    halo = u_ref[tm:tm + CONV_HALO, :]
    u_ref[0:CONV_HALO, :] = halo
    return _silu(_layer_norm(acc + b_ref[...], lng_ref[...], lnb_ref[...]))


def _in_proj_kernel(x_ref, mod_ref, w_rw_ref, w_mla_ref, w_conv_ref,
                    cw_ref, cb_ref, clng_ref, clnb_ref,
                    rw_ref, mla_ref, yconv_ref, u_ref, *, mod_base, per_b):
    @pl.when(pl.program_id(0) % per_b == 0)
    def _():
        u_ref[0:CONV_HALO, :] = jnp.zeros((CONV_HALO, u_ref.shape[1]), F32)

    sh = mod_ref[0, mod_base:mod_base + 1, :]
    sc = mod_ref[0, mod_base + 1:mod_base + 2, :]
    h = (x_ref[...] * (1.0 + sc) + sh).astype(BF16)
    p_conv = _dot(h, w_conv_ref[0])
    rw_ref[...] = _dot(h, w_rw_ref[0])
    mla_ref[...] = _dot(h, w_mla_ref[0])
    yconv_ref[...] = _conv_branch(p_conv, u_ref, cw_ref, cb_ref, clng_ref,
                                  clnb_ref).astype(yconv_ref.dtype)


def _in_proj(x, mod, w_in, conv_w, conv_b, conv_ln_g, conv_ln_b, *, layer, cols, mod_base, seq):
    t, d = x.shape
    rw_cols, mla_cols, conv_cols = cols
    ch = conv_w.shape[1]
    assert conv_cols == 2 * ch
    tm = min(TOKEN_TILE, seq)
    per_b = seq // tm
    w_mla = w_in[layer:layer + 1, :, rw_cols:rw_cols + mla_cols]
    w_conv = w_in[layer:layer + 1, :, rw_cols + mla_cols:]
    tok = lambda n: pl.BlockSpec((tm, n), lambda i: (i, 0))
    full = lambda a: pl.BlockSpec(a.shape, lambda i: (0, 0))
    resident = lambda n, l: pl.BlockSpec((1, d, n), lambda i: (l, 0, 0),
                                         pipeline_mode=pl.Buffered(1))
    row = lambda a: a.reshape(1, -1)
    small = (conv_w, row(conv_b), row(conv_ln_g), row(conv_ln_b))
    kern = functools.partial(_in_proj_kernel, mod_base=mod_base, per_b=per_b)
    return pl.pallas_call(
        kern,
        grid=(t // tm,),
        in_specs=[tok(d), pl.BlockSpec((1, N_MOD, d), lambda i: (i // per_b, 0, 0)),
                  resident(rw_cols, layer), resident(mla_cols, 0), resident(conv_cols, 0),
                  *[full(a) for a in small]],
        out_specs=[tok(rw_cols), tok(mla_cols), tok(ch)],
        out_shape=[jax.ShapeDtypeStruct((t, rw_cols), F32),
                   jax.ShapeDtypeStruct((t, mla_cols), F32),
                   jax.ShapeDtypeStruct((t, ch), BF16)],
        scratch_shapes=[pltpu.VMEM((CONV_HALO + tm, ch), F32)],
        compiler_params=_cparams("arbitrary"),
        name="in_proj",
    )(x, mod, w_in, w_mla, w_conv, *small)


def _rwkv_kernel(r_ref, k_ref, v_ref, lo_ref, mur_ref, muk_ref, muv_ref, mulo_ref,
                 w0_ref, w2_ref, a0_ref, a2_ref, g2_ref, kk_ref, ka_ref, rk_ref,
                 lng_ref, lnb_ref, o_ref,
                 state_ref, pr_ref, pk_ref, pv_ref, plo_ref, obuf_ref, *, bb, tb):
    j = pl.program_id(1)
    L = RW_CHUNK
    nc = tb // L

    @pl.when(j == 0)
    def _():
        state_ref[...] = jnp.zeros_like(state_ref)
        pr_ref[...] = jnp.zeros_like(pr_ref)
        pk_ref[...] = jnp.zeros_like(pk_ref)
        pv_ref[...] = jnp.zeros_like(pv_ref)
        plo_ref[...] = jnp.zeros_like(plo_ref)

    lane_r = lax.broadcasted_iota(jnp.int32, (LANES, LANES), 0)
    lane_c = lax.broadcasted_iota(jnp.int32, (LANES, LANES), 1)

    def same_block(size):
        return (lane_r // size) == (lane_c // size)

    same_head = same_block(RW_HEAD_DIM).astype(BF16)
    stack_mask = ((lane_r // L) == (lane_c // RW_HEAD_DIM)).astype(F32)
    strict_lower = lane_r > lane_c
    lower = lane_r >= lane_c
    eye = (lane_r == lane_c).astype(F32)
    tri = (lax.broadcasted_iota(jnp.int32, (L, L), 0)
           >= lax.broadcasted_iota(jnp.int32, (L, L), 1)).astype(BF16)
    zeros = jnp.zeros((LANES, LANES), F32)

    def head_sum(x):
        return _dot(x.astype(BF16), same_head)

    def stack(x):
        return jnp.concatenate([x, x], axis=0) * stack_mask

    def prologue(b):
        def shift_mix(p_ref, prev_ref, mu_ref):
            p = p_ref[b]
            row = lax.broadcasted_iota(jnp.int32, p.shape, 0)
            prev = jnp.where(row == 0, prev_ref[b, 0:1, :], pltpu.roll(p, 1, axis=0))
            prev_ref[b, 0:1, :] = p[tb - 1:tb, :]
            return p + (prev - p) * mu_ref[...]

        r = shift_mix(r_ref, pr_ref, mur_ref)
        k = shift_mix(k_ref, pk_ref, muk_ref)
        v = shift_mix(v_ref, pv_ref, muv_ref)
        lo = shift_mix(lo_ref, plo_ref, mulo_ref)
        w_lo = lo[:, :RW_DECAY_LORA]
        a_lo = lo[:, RW_DECAY_LORA:RW_DECAY_LORA + RW_AAA_LORA]
        g_lo = lo[:, RW_DECAY_LORA + RW_AAA_LORA:]
        z = w0_ref[...] + _rw_dot(jnp.tanh(w_lo), w2_ref[...])
        softplus_neg_z = jnp.maximum(-z, 0.0) + jnp.log(1.0 + jnp.exp(-jnp.abs(z)))
        log_decay = -jnp.exp(-softplus_neg_z - 0.5)
        a = _sigmoid(a0_ref[...] + _rw_dot(a_lo, a2_ref[...]))
        gate = _rw_dot(_sigmoid(g_lo), g2_ref[...])
        kk = k * kk_ref[...]
        kk = kk / jnp.maximum(jnp.sqrt(head_sum(kk * kk)), 1e-12)
        k = k * (1.0 + (a - 1.0) * ka_ref[...])
        return r, k, v, kk, a, gate, log_decay

    seqs = [prologue(b) for b in range(bb)]
    items = [(b, c) for c in range(nc) for b in range(bb)]
    every = range(len(items))

    def chunk_operands(b, c):
        r, k, v, kk, a, _, log_decay = seqs[b]
        rows = slice(c * L, (c + 1) * L)
        lw = log_decay[rows]
        cum = _dot_split_rhs(tri, lw, 3)
        w_cum = jnp.exp(cum)
        w_inv = jnp.exp(-cum)
        w_prev = jnp.exp(cum - lw)
        w_last = w_cum[L - 1:L, :]
        kk_c = kk[rows]
        return (stack(-kk_c * w_prev), stack(kk_c * a[rows] * w_inv), stack(k[rows] * w_inv),
                stack(r[rows] * w_cum), stack(v[rows]), w_last)

    ops = [chunk_operands(b, c) for b, c in items]
    a2_ = [o[0] for o in ops]
    b2_ = [o[1] for o in ops]
    k2_ = [o[2] for o in ops]
    r2_ = [o[3] for o in ops]
    v2_ = [o[4] for o in ops]
    w_last = [o[5] for o in ops]

    scores = [_dot_nt(jnp.concatenate([a2_[i], r2_[i]], axis=0).astype(BF16),
                      jnp.concatenate([b2_[i], k2_[i]], axis=0).astype(BF16)) for i in every]
    a_ab = [jnp.where(strict_lower, s[:LANES, :LANES], 0.0) for s in scores]
    a_ak = [jnp.where(strict_lower, s[:LANES, LANES:], 0.0) for s in scores]
    a_rb = [jnp.where(lower, s[LANES:, :LANES], 0.0) for s in scores]
    a_rk = [jnp.where(lower, s[LANES:, LANES:], 0.0) for s in scores]

    diag_blocks = same_block(RW_INV_BASE)
    p = [jnp.where(diag_blocks, x, 0.0) for x in a_ab]
    t = [eye + x for x in p]
    n = 2
    while n < RW_INV_BASE:
        p = [_mm3(x, x) for x in p]
        t = [t[i] + _mm3(t[i], p[i]) for i in every]
        n *= 2
    size = 2 * RW_INV_BASE
    while size <= L:
        off_diag = same_block(size) & ~same_block(size // 2)
        et = [_mm3(jnp.where(off_diag, a_ab[i], 0.0), t[i]) for i in every]
        t = [t[i] + _mm3(t[i], et[i]) for i in every]
        size *= 2

    akv = [_rw_dot(a_ak[i], v2_[i]) for i in every]
    x = [_mm3(t[i], jnp.concatenate([a2_[i], akv[i]], axis=1)) for i in every]
    big = []
    for i in every:
        lhs = jnp.concatenate(
            [jnp.concatenate([a_rb[i], a_rk[i]], axis=1),
             jnp.concatenate([(b2_[i] * w_last[i]).T, (k2_[i] * w_last[i]).T], axis=1)], axis=0)
        rhs = jnp.concatenate([x[i], jnp.concatenate([zeros, v2_[i]], axis=1)], axis=0)
        big.append(_mm3(lhs, rhs))

    states = [state_ref[b] for b in range(bb)]
    for i, (b, c) in enumerate(items):
        r_hat = r2_[i] + big[i][:LANES, :LANES]
        st = _rw_dot(jnp.concatenate([r_hat, big[i][LANES:, :LANES]], axis=0), states[b])
        o2 = st[:LANES] + big[i][:LANES, LANES:]
        obuf_ref[b, c * L:(c + 1) * L, :] = o2[:L] + o2[L:]
        w_rows = jnp.broadcast_to(w_last[i], (LANES, LANES)).T
        states[b] = states[b] * w_rows + st[LANES:] + big[i][LANES:, LANES:]
    for b in range(bb):
        state_ref[b] = states[b]

    inv_n = 1.0 / RW_HEAD_DIM
    for b in range(bb):
        r, k, v, _, _, gate, _ = seqs[b]
        o = obuf_ref[b]
        mean = head_sum(o) * inv_n
        d = o - mean
        var = head_sum(d * d) * inv_n
        on = d * lax.rsqrt(var + RW_GN_EPS) * lng_ref[...] + lnb_ref[...]
        bonus = head_sum(r * k * rk_ref[...]) * v
        o_ref[b] = ((on + bonus) * gate).astype(o_ref.dtype)


def _rwkv(p_rw, mu, w0, w2, a0, a2, g2, k_k, k_a, r_k, lnx_g, lnx_b, *, batch, seq):
    width = w0.shape[-1]
    pairs = width // LANES
    tb = min(RW_BLOCK, seq)
    lo_w = RW_DECAY_LORA + RW_AAA_LORA + RW_GATE_LORA
    lo_blk = 3 * width // lo_w
    p3 = p_rw.reshape(batch, seq, p_rw.shape[-1])

    def col(off):
        return lambda h, j: (0, j, off + h)

    def vec(off):
        return lambda h, j: (0, off + h)

    row = lambda a: a.reshape(1, -1)
    kern = functools.partial(_rwkv_kernel, bb=batch, tb=tb)
    out = pl.pallas_call(
        kern,
        grid=(pairs, seq // tb),
        in_specs=[pl.BlockSpec((batch, tb, LANES), col(0)),
                  pl.BlockSpec((batch, tb, LANES), col(pairs)),
                  pl.BlockSpec((batch, tb, LANES), col(2 * pairs)),
                  pl.BlockSpec((batch, tb, lo_w), lambda h, j: (0, j, lo_blk)),
                  pl.BlockSpec((1, LANES), vec(0)),
                  pl.BlockSpec((1, LANES), vec(pairs)),
                  pl.BlockSpec((1, LANES), vec(2 * pairs)),
                  pl.BlockSpec((1, lo_w), lambda h, j: (0, lo_blk)),
                  pl.BlockSpec((1, LANES), vec(0)),
                  pl.BlockSpec((RW_DECAY_LORA, LANES), vec(0)),
                  pl.BlockSpec((1, LANES), vec(0)),
                  pl.BlockSpec((RW_AAA_LORA, LANES), vec(0)),
                  pl.BlockSpec((RW_GATE_LORA, LANES), vec(0)),
                  pl.BlockSpec((1, LANES), vec(0)),
                  pl.BlockSpec((1, LANES), vec(0)),
                  pl.BlockSpec((1, LANES), vec(0)),
                  pl.BlockSpec((1, LANES), vec(0)),
                  pl.BlockSpec((1, LANES), vec(0))],
        out_specs=pl.BlockSpec((batch, tb, LANES), lambda h, j: (0, j, h)),
        out_shape=jax.ShapeDtypeStruct((batch, seq, width), RW_OUT_DTYPE),
        scratch_shapes=[pltpu.VMEM((batch, LANES, LANES), F32),
                        pltpu.VMEM((batch, SUBLANES, LANES), F32),
                        pltpu.VMEM((batch, SUBLANES, LANES), F32),
                        pltpu.VMEM((batch, SUBLANES, LANES), F32),
                        pltpu.VMEM((batch, SUBLANES, lo_w), F32),
                        pltpu.VMEM((batch, tb, LANES), F32)],
        compiler_params=_cparams("parallel", "arbitrary"),
        name="rwkv",
    )(p3, p3, p3, p3, row(mu), row(mu), row(mu), row(mu), row(w0), w2, row(a0), a2, g2,
      row(k_k), row(k_a), row(r_k), row(lnx_g), row(lnx_b))
    return out.reshape(batch * seq, width)


def _rope_table_kernel(pos_ref, freq_ref, cos_ref, sa_ref, sb_ref):
    half = MLA_ROPE // 2
    ang = pos_ref[...].astype(F32) * freq_ref[...]
    lane = lax.broadcasted_iota(jnp.int32, ang.shape, 1)
    cos = jnp.cos(ang)
    sin = jnp.sin(ang)
    cos_ref[...] = jnp.where(lane < MLA_ROPE, cos, 1.0)
    sa_ref[...] = jnp.where(lane < half, -sin, 0.0)
    sb_ref[...] = jnp.where((lane >= half) & (lane < MLA_ROPE), sin, 0.0)


def _rope_tables(positions):
    t = positions.size
    half = MLA_ROPE // 2
    tm = min(2048, t)
    inv_freq = ROPE_THETA ** (-jnp.arange(half, dtype=F32) / half)
    freq = jnp.concatenate([inv_freq, inv_freq, jnp.zeros((LANES - MLA_ROPE,), F32)])
    spec = pl.BlockSpec((tm, LANES), lambda i: (i, 0))
    shp = jax.ShapeDtypeStruct((t, LANES), F32)
    return pl.pallas_call(
        _rope_table_kernel,
        grid=(t // tm,),
        in_specs=[pl.BlockSpec((tm, 1), lambda i: (i, 0)),
                  pl.BlockSpec((1, LANES), lambda i: (0, 0))],
        out_specs=[spec, spec, spec],
        out_shape=[shp, shp, shp],
        compiler_params=_cparams("parallel"),
        name="rope_tables",
    )(positions.reshape(t, 1), freq.reshape(1, LANES))


def _rope(x, cos, sa, sb):
    half = MLA_ROPE // 2
    return x * cos + pltpu.roll(x, LANES - half, axis=1) * sa + pltpu.roll(x, half, axis=1) * sb


def _mla_up_kernel(p_ref, cos_ref, sa_ref, sb_ref, qg_ref, wq_ref, kvg_ref, wk_ref, wvt_ref,
                   q_ref, k_ref, vt_ref, *, heads):
    p = p_ref[...]
    cos, sa, sb = cos_ref[...], sa_ref[...], sb_ref[...]

    def rms(x, g):
        return x * lax.rsqrt(jnp.mean(x * x, -1, keepdims=True) + RMS_EPS) * g

    q_lat = rms(p[:, :MLA_Q_LORA], qg_ref[...]).astype(BF16)
    kv_lat = rms(p[:, MLA_Q_LORA:MLA_Q_LORA + MLA_KV_LORA], kvg_ref[...]).astype(BF16)
    k_pe = p[:, MLA_Q_LORA + MLA_KV_LORA:]
    k_pe = jnp.concatenate([k_pe, jnp.zeros((k_pe.shape[0], LANES - MLA_ROPE), F32)], axis=1)
    k_pe = _rope(k_pe, cos, sa, sb).astype(BF16)

    q = _dot(q_lat, wq_ref[...]) * (MLA_SCALE * LOG2_E)
    k_nope = _dot(kv_lat, wk_ref[...])
    vt_ref[0] = _dot_nt(wvt_ref[...], kv_lat).astype(BF16)
    for h in range(heads):
        base = h * MLA_QK_PAD
        q_ref[:, base:base + MLA_NOPE] = q[:, base:base + MLA_NOPE].astype(BF16)
        q_ref[:, base + MLA_NOPE:base + MLA_QK_PAD] = _rope(
            q[:, base + MLA_NOPE:base + MLA_QK_PAD], cos, sa, sb).astype(BF16)
        k_ref[:, base:base + MLA_NOPE] = k_nope[:, h * MLA_NOPE:(h + 1) * MLA_NOPE].astype(BF16)
        k_ref[:, base + MLA_NOPE:base + MLA_QK_PAD] = k_pe


def _mla_up(p_mla, tables, q_norm_g, w_uq, kv_norm_g, w_ukv, *, seq):
    t, cols = p_mla.shape
    heads = w_uq.shape[1] // (MLA_NOPE + MLA_ROPE)
    tm = min(TOKEN_TILE, seq)
    wq = w_uq.reshape(MLA_Q_LORA, heads, MLA_NOPE + MLA_ROPE)
    wq = jnp.pad(wq, ((0, 0), (0, 0), (0, MLA_QK_PAD - MLA_NOPE - MLA_ROPE)))
    wq = wq.reshape(MLA_Q_LORA, heads * MLA_QK_PAD).astype(BF16)
    wkv = w_ukv.reshape(MLA_KV_LORA, heads, MLA_NOPE + MLA_V)
    wk = wkv[:, :, :MLA_NOPE].reshape(MLA_KV_LORA, heads * MLA_NOPE).astype(BF16)
    wvt = wkv[:, :, MLA_NOPE:].reshape(MLA_KV_LORA, heads * MLA_V).T.astype(BF16)
    tok = lambda n: pl.BlockSpec((tm, n), lambda i: (i, 0))
    full = lambda a: pl.BlockSpec(a.shape, lambda i: (0, 0))
    qg = q_norm_g.reshape(1, -1)
    kvg = kv_norm_g.reshape(1, -1)
    kern = functools.partial(_mla_up_kernel, heads=heads)
    return pl.pallas_call(
        kern,
        grid=(t // tm,),
        in_specs=[tok(cols), tok(LANES), tok(LANES), tok(LANES),
                  full(qg), full(wq), full(kvg), full(wk), full(wvt)],
        out_specs=[tok(heads * MLA_QK_PAD), tok(heads * MLA_QK_PAD),
                   pl.BlockSpec((1, heads * MLA_V, tm), lambda i: (i, 0, 0))],
        out_shape=[jax.ShapeDtypeStruct((t, heads * MLA_QK_PAD), BF16),
                   jax.ShapeDtypeStruct((t, heads * MLA_QK_PAD), BF16),
                   jax.ShapeDtypeStruct((t // tm, heads * MLA_V, tm), BF16)],
        compiler_params=_cparams("parallel"),
        name="mla_up",
    )(p_mla, *tables, qg, wq, kvg, wk, wvt)


def _attn_kernel(qi_ref, kj_ref, q_ref, k_ref, vt_ref, o_ref, m_ref, l_ref, acc_ref, *,
                 heads, tile):
    step_id = pl.program_id(1)
    i = qi_ref[step_id]
    j = kj_ref[step_id]

    @pl.when(j == 0)
    def _():
        m_ref[...] = jnp.full_like(m_ref, -jnp.inf)
        l_ref[...] = jnp.zeros_like(l_ref)
        acc_ref[...] = jnp.zeros_like(acc_ref)

    def qk(h):
        return slice(h * MLA_QK_PAD, (h + 1) * MLA_QK_PAD)

    def vo(h):
        return slice(h * MLA_V, (h + 1) * MLA_V)

    def step(masked, sub):
        hs = range(heads)
        qs = slice(sub * tile, (sub + 1) * tile)
        s = [_dot_nt(k_ref[:, qk(h)], q_ref[qs, qk(h)]) for h in hs]
        if masked:
            key = lax.broadcasted_iota(jnp.int32, s[0].shape, 0)
            query = lax.broadcasted_iota(jnp.int32, s[0].shape, 1)
            s = [jnp.where(key <= query, x, -jnp.inf) for x in s]
        m_prev = [m_ref[h, :, qs] for h in hs]
        m_new = [jnp.maximum(m_prev[h], jnp.max(s[h], 0, keepdims=True)) for h in hs]
        alpha = [jnp.exp2(m_prev[h] - m_new[h]) for h in hs]
        p = [jnp.exp2(s[h] - m_new[h]) for h in hs]
        pv = [_dot(vt_ref[0, vo(h), :], p[h].astype(BF16)) for h in hs]
        for h in hs:
            l_ref[h, :, qs] = alpha[h] * l_ref[h, :, qs] + jnp.sum(p[h], 0, keepdims=True)
            acc_ref[vo(h), qs] = alpha[h] * acc_ref[vo(h), qs] + pv[h]
            m_ref[h, :, qs] = m_new[h]

    for sub in range(ATTN_Q_TILES):
        diagonal = i * ATTN_Q_TILES + sub
        pl.when(j < diagonal)(functools.partial(step, False, sub))
        pl.when(j == diagonal)(functools.partial(step, True, sub))

    @pl.when(j == (i + 1) * ATTN_Q_TILES - 1)
    def _():
        for h in range(heads):
            o_ref[:, vo(h)] = (acc_ref[vo(h), :] / l_ref[h]).T.astype(o_ref.dtype)


def _attention(q, k, vt, *, batch, seq):
    t = q.shape[0]
    heads = vt.shape[1] // MLA_V
    tile = min(ATTN_TILE, seq)
    assert vt.shape[2] == tile
    tq = ATTN_Q_TILES * tile
    nq = seq // tq
    nk = seq // tile
    assert nq * tq == seq
    pairs = [(i, j) for i in range(nq) for j in range((i + 1) * ATTN_Q_TILES)]
    qi = jnp.asarray([p[0] for p in pairs], jnp.int32)
    kj = jnp.asarray([p[1] for p in pairs], jnp.int32)
    kern = functools.partial(_attn_kernel, heads=heads, tile=tile)
    grid_spec = pltpu.PrefetchScalarGridSpec(
        num_scalar_prefetch=2,
        grid=(batch, len(pairs)),
        in_specs=[pl.BlockSpec((tq, heads * MLA_QK_PAD), lambda b, s, qi, kj: (b * nq + qi[s], 0)),
                  pl.BlockSpec((tile, heads * MLA_QK_PAD),
                               lambda b, s, qi, kj: (b * nk + kj[s], 0)),
                  pl.BlockSpec((1, heads * MLA_V, tile),
                               lambda b, s, qi, kj: (b * nk + kj[s], 0, 0))],
        out_specs=pl.BlockSpec((tq, heads * MLA_V), lambda b, s, qi, kj: (b * nq + qi[s], 0)),
        scratch_shapes=[pltpu.VMEM((heads, 1, tq), F32), pltpu.VMEM((heads, 1, tq), F32),
                        pltpu.VMEM((heads * MLA_V, tq), F32)])
    return pl.pallas_call(
        kern,
        grid_spec=grid_spec,
        out_shape=jax.ShapeDtypeStruct((t, heads * MLA_V), BF16),
        compiler_params=_cparams("parallel", "arbitrary"),
        name="attention",
    )(qi, kj, q, k, vt)


def _out_proj_kernel(x_ref, mod_ref, y1_ref, y2_ref, y3_ref, w1_ref, w2_ref, w3_ref,
                     lng_ref, lnb_ref, o_ref, *, mod_base, alpha):
    y = (_dot(y1_ref[...], w1_ref[0]) + _dot(y2_ref[...], w2_ref[0])
         + _dot(y3_ref[...], w3_ref[0]))
    g = mod_ref[0, mod_base + 2:mod_base + 3, :]
    o_ref[...] = _layer_norm(alpha * x_ref[...] + (1.0 + g) * y, lng_ref[...], lnb_ref[...])


def _out_proj(x, mod, ys, w_out, ln_g, ln_b, *, layer, mod_base, alpha, seq):
    t, d = x.shape
    tm = min(TOKEN_TILE, seq)
    per_b = seq // tm
    tok = lambda n: pl.BlockSpec((tm, n), lambda i: (i, 0))
    full = lambda a: pl.BlockSpec(a.shape, lambda i: (0, 0))
    w_specs = []
    offset = 0
    for y in ys:
        width = y.shape[1]
        assert offset % width == 0
        w_specs.append(pl.BlockSpec((1, width, d), functools.partial(
            lambda i, blk: (layer, blk, 0), blk=offset // width)))
        offset += width
    assert offset == w_out.shape[1]
    kern = functools.partial(_out_proj_kernel, mod_base=mod_base, alpha=alpha)
    return pl.pallas_call(
        kern,
        grid=(t // tm,),
        in_specs=[tok(d), pl.BlockSpec((1, N_MOD, d), lambda i: (i // per_b, 0, 0)),
                  *[tok(y.shape[1]) for y in ys], *w_specs, full(ln_g), full(ln_b)],
        out_specs=tok(d),
        out_shape=jax.ShapeDtypeStruct((t, d), F32),
        compiler_params=_cparams("parallel"),
        name="out_proj",
    )(x, mod, *ys, w_out, w_out, w_out, ln_g, ln_b)


def kernel(x, c, positions, w_ada, b_ada, ln_g, ln_b, w_ffn1_in, w_ffn1_out, w_ffn2_in, w_ffn2_out, w_in, w_out, rw_mu, rw_w0, rw_w2, rw_a0, rw_a2, rw_g2, rw_k_k, rw_k_a, rw_r_k, rw_lnx_g, rw_lnx_b, mla_q_norm_g, mla_w_uq, mla_kv_norm_g, mla_w_ukv, conv_w, conv_b, conv_ln_g, conv_ln_b):
    batch, seq, d = x.shape
    depth = w_ada.shape[0]
    alpha = (2 * depth) ** 0.25
    rw_cols = rw_mu.shape[1]
    mla_cols = MLA_Q_LORA + MLA_KV_LORA + MLA_ROPE
    conv_cols = w_in.shape[2] - rw_cols - mla_cols

    mod_all = _ada(c, w_ada, b_ada).reshape(depth, batch, N_MOD, d)
    tables = _rope_tables(positions)
    xt = x.reshape(batch * seq, d)
    row = lambda a: a.reshape(1, -1)
    w_ffn1_in, w_ffn1_out, w_ffn2_in, w_ffn2_out, w_in, w_out = (
        w.astype(BF16) for w in (w_ffn1_in, w_ffn1_out, w_ffn2_in, w_ffn2_out, w_in, w_out))

    for l in range(depth):
        mod = mod_all[l]
        xt = _ffn(xt, mod, w_ffn1_in, w_ffn1_out, row(ln_g[l, 0]), row(ln_b[l, 0]),
                  layer=l, mod_base=0, alpha=alpha, seq=seq)

        p_rw, p_mla, y_conv = _in_proj(xt, mod, w_in, conv_w[l], conv_b[l], conv_ln_g[l],
                                       conv_ln_b[l], layer=l,
                                       cols=(rw_cols, mla_cols, conv_cols), mod_base=3, seq=seq)

        y_rw = _rwkv(p_rw, rw_mu[l], rw_w0[l], rw_w2[l], rw_a0[l], rw_a2[l], rw_g2[l],
                     rw_k_k[l], rw_k_a[l], rw_r_k[l], rw_lnx_g[l], rw_lnx_b[l],
                     batch=batch, seq=seq)
        q, k, v = _mla_up(p_mla, tables, mla_q_norm_g[l], mla_w_uq[l], mla_kv_norm_g[l],
                          mla_w_ukv[l], seq=seq)
        y_mla = _attention(q, k, v, batch=batch, seq=seq)

        xt = _out_proj(xt, mod, (y_rw, y_mla, y_conv), w_out, row(ln_g[l, 1]), row(ln_b[l, 1]),
                       layer=l, mod_base=3, alpha=alpha, seq=seq)

        xt = _ffn(xt, mod, w_ffn2_in, w_ffn2_out, row(ln_g[l, 2]), row(ln_b[l, 2]),
                  layer=l, mod_base=6, alpha=alpha, seq=seq)
    return xt.reshape(batch, seq, d)
```
